```python
import math
import jax, jax.numpy as jnp
from jax import lax
import numpy as np

D_MODEL = 1024
BATCH = 8
SEQ = 2048
DEPTH = 4

N_MIXERS = 3
EPS = 1e-6
CHUNK = 128
E_GMLP = 2 * D_MODEL
N_GROUPS_A = 8
GA_DIM = E_GMLP // N_GROUPS_A
S5_GROUP = 16
S5_GROUPS = D_MODEL // S5_GROUP
S5_STATE = 64
DT_MIN = 1e-3
DT_MAX = 1e-1
D_RNN = 3 * D_MODEL // 2
N_HEADS_C = 12
HC_DIM = D_RNN // N_HEADS_C
CONV_W = 4
LRU_C = 8.0
D_FF = 11 * D_MODEL // 4
N_EXPERTS = 8
TOP_K = 2
D_FF_EXPERT = 7 * D_MODEL // 2
N_A = (DEPTH + 2) // 3
N_B = (DEPTH + 1) // 3
N_C = DEPTH // 3
N_DENSE = (DEPTH + 1) // 2
N_MOE = DEPTH // 2

kernel_name = "hybrid_gmlp_s5_rglru_moe_trunk"


def rmsnorm(x, g):
    xf = x.astype(jnp.float32)
    y = xf * lax.rsqrt(jnp.mean(xf * xf, axis=-1, keepdims=True) + EPS)
    return (y * g.astype(jnp.float32)).astype(x.dtype)


def layernorm(x, g, b):
    xf = x.astype(jnp.float32)
    mu = jnp.mean(xf, axis=-1, keepdims=True)
    xc = xf - mu
    y = xc * lax.rsqrt(jnp.mean(xc * xc, axis=-1, keepdims=True) + EPS)
    return (y * g.astype(jnp.float32) + b.astype(jnp.float32)).astype(x.dtype)


def mixer_gmlp(h, w_in, ln_g, ln_b, w_s, b_s, w_out):
    bsz, seq, _ = h.shape
    z = jax.nn.gelu(h @ w_in)
    u, v = jnp.split(z, 2, axis=-1)
    v = layernorm(v, ln_g, ln_b)
    v = v.reshape(bsz, seq // CHUNK, CHUNK, N_GROUPS_A, GA_DIM)
    mask = jnp.tril(jnp.ones((CHUNK, CHUNK), dtype=bool))
    ws = jnp.where(mask[None], w_s, 0.0).astype(v.dtype)
    sv = jnp.einsum('gts,bcsge->bctge', ws, v) + b_s.T.astype(v.dtype)[:, :, None]
    y = u * sv.reshape(bsz, seq, E_GMLP)
    return y @ w_out


def _complex_scan_combine(e1, e2):
    a1r, a1i, b1r, b1i = e1
    a2r, a2i, b2r, b2i = e2
    return (a2r * a1r - a2i * a1i,
            a2r * a1i + a2i * a1r,
            a2r * b1r - a2i * b1i + b2r,
            a2r * b1i + a2i * b1r + b2i)


def mixer_s5(h, a_re, a_im, log_dt, b_re, b_im, c_re, c_im, d_skip, w_glu):
    f32 = jnp.float32
    bsz, seq, _ = h.shape
    dt = jnp.exp(log_dt.astype(f32))[:, None]
    ar = a_re.astype(f32)
    ai = a_im.astype(f32)
    mag = jnp.exp(ar * dt)
    lr = mag * jnp.cos(ai * dt)
    li = mag * jnp.sin(ai * dt)
    den = ar * ar + ai * ai
    nr = lr - 1.0
    kr = (nr * ar + li * ai) / den
    ki = (li * ar - nr * ai) / den
    br = b_re.astype(f32)
    bi = b_im.astype(f32)
    bbr = kr[..., None] * br - ki[..., None] * bi
    bbi = kr[..., None] * bi + ki[..., None] * br
    u = h.astype(f32).reshape(bsz, seq, S5_GROUPS, S5_GROUP)
    xr = jnp.einsum('blgh,gph->blgp', u, bbr)
    xi = jnp.einsum('blgh,gph->blgp', u, bbi)
    lam_r = jnp.broadcast_to(lr, xr.shape)
    lam_i = jnp.broadcast_to(li, xi.shape)
    _, _, sr, si = lax.associative_scan(_complex_scan_combine, (lam_r, lam_i, xr, xi), axis=1)
    y = (jnp.einsum('ghp,blgp->blgh', c_re.astype(f32), sr)
         - jnp.einsum('ghp,blgp->blgh', c_im.astype(f32), si))
    y = y.reshape(bsz, seq, D_MODEL) + d_skip.astype(f32) * h.astype(f32)
    y = jax.nn.gelu(y).astype(h.dtype)
    a, b = jnp.split(y @ w_glu, 2, axis=-1)
    return a * jax.nn.sigmoid(b)


def _linear_scan_combine(e1, e2):
    a1, b1 = e1
    a2, b2 = e2
    return (a1 * a2, a2 * b1 + b2)


def mixer_rglru(h, w_in, conv_w, conv_b, w_r, b_r, w_i, b_i, lam, w_out):
    f32 = jnp.float32
    bsz, seq, _ = h.shape
    xb, gb = jnp.split(h @ w_in, 2, axis=-1)
    gate = jax.nn.gelu(gb)
    xc = lax.conv_general_dilated(
        xb, conv_w[:, None, :].astype(xb.dtype), window_strides=(1,),
        padding=[(CONV_W - 1, 0)], dimension_numbers=('NWC', 'WIO', 'NWC'),
        feature_group_count=D_RNN) + conv_b.astype(xb.dtype)
    xh = xc.reshape(bsz, seq, N_HEADS_C, HC_DIM)
    r = jax.nn.sigmoid((jnp.einsum('blhi,hij->blhj', xh, w_r).reshape(bsz, seq, D_RNN)
                        + b_r).astype(f32))
    ig = jax.nn.sigmoid((jnp.einsum('blhi,hij->blhj', xh, w_i).reshape(bsz, seq, D_RNN)
                         + b_i).astype(f32))
    log_a = -LRU_C * r * jax.nn.softplus(-lam.astype(f32))
    a = jnp.exp(log_a)
    bx = jnp.sqrt(-jnp.expm1(2.0 * log_a)) * (ig * xc.astype(f32))
    _, hs = lax.associative_scan(_linear_scan_combine, (a, bx), axis=1)
    return (hs.astype(h.dtype) * gate) @ w_out


def swiglu(h, w_gu, w_down):
    g, u = jnp.split(h @ w_gu, 2, axis=-1)
    return (jax.nn.silu(g) * u) @ w_down


def moe_swiglu(h, w_router, w_gu, w_down):
    bsz, seq, d = h.shape
    t = h.reshape(bsz * seq, d)
    logits = (t @ w_router).astype(jnp.float32)
    vals, idx = lax.top_k(logits, TOP_K)
    gates = jax.nn.softmax(vals, axis=-1)
    dense_gates = jnp.sum(jax.nn.one_hot(idx, N_EXPERTS, dtype=jnp.float32) * gates[..., None],
                          axis=1).astype(h.dtype)
    out = jnp.zeros_like(t)
    for e in range(N_EXPERTS):
        out = out + dense_gates[:, e:e + 1] * swiglu(t, w_gu[e], w_down[e])
    return out.reshape(bsz, seq, d)


def setup_inputs(seed: int = 0) -> dict:
    key = jax.random.key(seed)
    ks = jax.random.split(key, 48)
    counter = iter(range(48))
    f32 = jnp.float32

    def nk():
        return ks[next(counter)]

    def nrm(shape, scale):
        return scale * jax.random.normal(nk(), shape, f32)

    x = nrm((BATCH, SEQ, D_MODEL), 1.0)
    norm_mix_g = 1.0 + nrm((DEPTH, D_MODEL), 0.01)
    norm_ffn_g = 1.0 + nrm((DEPTH, D_MODEL), 0.01)
    norm_final_g = 1.0 + nrm((D_MODEL,), 0.01)
    a_w_in = nrm((N_A, D_MODEL, 2 * E_GMLP), D_MODEL ** -0.5)
    a_ln_g = 1.0 + nrm((N_A, E_GMLP), 0.01)
    a_ln_b = nrm((N_A, E_GMLP), 0.01)
    a_w_s = nrm((N_A, N_GROUPS_A, CHUNK, CHUNK), CHUNK ** -0.5)
    a_b_s = 1.0 + nrm((N_A, N_GROUPS_A, CHUNK), 0.1)
    a_w_out = nrm((N_A, E_GMLP, D_MODEL), E_GMLP ** -0.5)
    b_a_re = -0.5 + nrm((N_B, S5_GROUPS, S5_STATE), 0.01)
    b_a_im = (jnp.pi * jnp.arange(S5_STATE, dtype=f32))[None, None, :] + nrm((N_B, S5_GROUPS, S5_STATE), 0.01)
    b_log_dt = jax.random.uniform(nk(), (N_B, S5_GROUPS), f32,
                                  minval=math.log(DT_MIN), maxval=math.log(DT_MAX))
    b_b_re = nrm((N_B, S5_GROUPS, S5_STATE, S5_GROUP), S5_GROUP ** -0.5)
    b_b_im = nrm((N_B, S5_GROUPS, S5_STATE, S5_GROUP), S5_GROUP ** -0.5)
    b_c_re = nrm((N_B, S5_GROUPS, S5_GROUP, S5_STATE), S5_STATE ** -0.5)
    b_c_im = nrm((N_B, S5_GROUPS, S5_GROUP, S5_STATE), S5_STATE ** -0.5)
    b_d = nrm((N_B, D_MODEL), 1.0)
    b_w_glu = nrm((N_B, D_MODEL, 2 * D_MODEL), D_MODEL ** -0.5)
    c_w_in = nrm((N_C, D_MODEL, 2 * D_RNN), D_MODEL ** -0.5)
    c_conv_w = nrm((N_C, CONV_W, D_RNN), CONV_W ** -0.5)
    c_conv_b = nrm((N_C, D_RNN), 0.01)
    c_w_r = nrm((N_C, N_HEADS_C, HC_DIM, HC_DIM), HC_DIM ** -0.5)
    c_b_r = nrm((N_C, D_RNN), 0.01)
    c_w_i = nrm((N_C, N_HEADS_C, HC_DIM, HC_DIM), HC_DIM ** -0.5)
    c_b_i = nrm((N_C, D_RNN), 0.01)
    a_pow = jax.random.uniform(nk(), (N_C, D_RNN), f32, minval=0.9, maxval=0.999)
    s = a_pow ** (1.0 / LRU_C)
    c_lam = jnp.log(s) - jnp.log1p(-s)
    c_w_out = nrm((N_C, D_RNN, D_MODEL), D_RNN ** -0.5)
    f_w_gu = nrm((N_DENSE, D_MODEL, 2 * D_FF), D_MODEL ** -0.5)
    f_w_down = nrm((N_DENSE, D_FF, D_MODEL), D_FF ** -0.5)
    m_w_router = nrm((N_MOE, D_MODEL, N_EXPERTS), D_MODEL ** -0.5)
    m_w_gu = nrm((N_MOE, N_EXPERTS, D_MODEL, 2 * D_FF_EXPERT), D_MODEL ** -0.5)
    m_w_down = nrm((N_MOE, N_EXPERTS, D_FF_EXPERT, D_MODEL), D_FF_EXPERT ** -0.5)
    return {
        "x": x, "norm_mix_g": norm_mix_g, "norm_ffn_g": norm_ffn_g, "norm_final_g": norm_final_g,
        "a_w_in": a_w_in, "a_ln_g": a_ln_g, "a_ln_b": a_ln_b, "a_w_s": a_w_s, "a_b_s": a_b_s,
        "a_w_out": a_w_out,
        "b_a_re": b_a_re, "b_a_im": b_a_im, "b_log_dt": b_log_dt, "b_b_re": b_b_re,
        "b_b_im": b_b_im, "b_c_re": b_c_re, "b_c_im": b_c_im, "b_d": b_d, "b_w_glu": b_w_glu,
        "c_w_in": c_w_in, "c_conv_w": c_conv_w, "c_conv_b": c_conv_b, "c_w_r": c_w_r,
        "c_b_r": c_b_r, "c_w_i": c_w_i, "c_b_i": c_b_i, "c_lam": c_lam, "c_w_out": c_w_out,
        "f_w_gu": f_w_gu, "f_w_down": f_w_down,
        "m_w_router": m_w_router, "m_w_gu": m_w_gu, "m_w_down": m_w_down,
    }


def reference(x, norm_mix_g, norm_ffn_g, norm_final_g,
              a_w_in, a_ln_g, a_ln_b, a_w_s, a_b_s, a_w_out,
              b_a_re, b_a_im, b_log_dt, b_b_re, b_b_im, b_c_re, b_c_im, b_d, b_w_glu,
              c_w_in, c_conv_w, c_conv_b, c_w_r, c_b_r, c_w_i, c_b_i, c_lam, c_w_out,
              f_w_gu, f_w_down,
              m_w_router, m_w_gu, m_w_down):
    h = x
    for i in range(DEPTH):
        hn = rmsnorm(h, norm_mix_g[i])
        kind = i % N_MIXERS
        j = i // N_MIXERS
        if kind == 0:
            y = mixer_gmlp(hn, a_w_in[j], a_ln_g[j], a_ln_b[j], a_w_s[j], a_b_s[j], a_w_out[j])
        elif kind == 1:
            y = mixer_s5(hn, b_a_re[j], b_a_im[j], b_log_dt[j], b_b_re[j], b_b_im[j],
                         b_c_re[j], b_c_im[j], b_d[j], b_w_glu[j])
        else:
            y = mixer_rglru(hn, c_w_in[j], c_conv_w[j], c_conv_b[j], c_w_r[j], c_b_r[j],
                            c_w_i[j], c_b_i[j], c_lam[j], c_w_out[j])
        h = h + y
        hn = rmsnorm(h, norm_ffn_g[i])
        k = i // 2
        if i % 2 == 0:
            h = h + swiglu(hn, f_w_gu[k], f_w_down[k])
        else:
            h = h + moe_swiglu(hn, m_w_router[k], m_w_gu[k], m_w_down[k])
    return rmsnorm(h, norm_final_g)
```

```python
import functools
import math

import jax
import jax.numpy as jnp
from jax import lax
from jax.experimental import pallas as pl
from jax.experimental.pallas import tpu as pltpu

F32 = jnp.float32
BF16 = jnp.bfloat16
EPS = 1e-6
N_MIXERS = 3
CHUNK = 128
N_GROUPS_A = 8
S5_GROUP = 16
S5_T = 32
N_HEADS_C = 12
LRU_C = 8.0
TOP_K = 2
LANES = 128
VMEM_LIMIT = 56 * 2**20


def _cparams(*sem):
    return pltpu.CompilerParams(dimension_semantics=sem, vmem_limit_bytes=VMEM_LIMIT)


def _resident(shape):
    nd = len(shape)
    return pl.BlockSpec(shape, lambda *_: (0,) * nd, pipeline_mode=pl.Buffered(1))


def _rms(x, g):
    return x * lax.rsqrt(jnp.mean(x * x, axis=-1, keepdims=True) + EPS) * g


def _gelu(x):
    c = math.sqrt(2.0 / math.pi)
    return x * (0.5 * (1.0 + jnp.tanh(c * (x + 0.044715 * (x * x * x)))))


def _dot(a, b):
    return jnp.dot(a, b, preferred_element_type=F32)


def _gmlp_in_kernel(h_ref, g_ref, w_ref, lng_ref, lnb_ref, u_ref, v_ref):
    e = u_ref.shape[-1]
    hn = _rms(h_ref[...], g_ref[...]).astype(BF16)
    u_ref[...] = _gelu(_dot(hn, w_ref[:, :e])).astype(BF16)
    v = _gelu(_dot(hn, w_ref[:, e:]))
    vc = v - jnp.mean(v, axis=-1, keepdims=True)
    var = jnp.mean(vc * vc, axis=-1, keepdims=True)
    v_ref[...] = (vc * lax.rsqrt(var + EPS) * lng_ref[...] + lnb_ref[...]).astype(BF16)


def _gmlp_out_kernel(u_ref, v_ref, ws_ref, bs_ref, wo_ref, h_ref, o_ref, y_scr):
    tc, e = u_ref.shape
    ng = ws_ref.shape[0]
    ge = e // ng
    for c in range(tc // CHUNK):
        r = slice(c * CHUNK, (c + 1) * CHUNK)
        for g in range(ng):
            cs = slice(g * ge, (g + 1) * ge)
            sv = _dot(ws_ref[g], v_ref[r, cs]) + bs_ref[:, g:g + 1]
            y_scr[r, cs] = (u_ref[r, cs].astype(F32) * sv).astype(BF16)
    o_ref[...] = h_ref[...] + _dot(y_scr[...], wo_ref[...])


def _mixer_gmlp(h, g, w_in, ln_g, ln_b, w_s, b_s, w_out, *, tm=512):
    n, d = h.shape
    e = w_out.shape[0]
    ng = w_s.shape[0]
    row = lambda i: (i, 0)
    u, v = pl.pallas_call(
        _gmlp_in_kernel,
        grid=(n // tm,),
        in_specs=[pl.BlockSpec((tm, d), row), _resident((1, d)), _resident((d, 2 * e)),
                  _resident((1, e)), _resident((1, e))],
        out_specs=[pl.BlockSpec((tm, e), row), pl.BlockSpec((tm, e), row)],
        out_shape=[jax.ShapeDtypeStruct((n, e), BF16)] * 2,
        compiler_params=_cparams("parallel"),
        name="gmlp_in",
    )(h, g.reshape(1, d), w_in.astype(BF16), ln_g.reshape(1, e), ln_b.reshape(1, e))
    mask = jnp.tril(jnp.ones((CHUNK, CHUNK), dtype=bool))
    ws = jnp.where(mask[None], w_s, 0.0).astype(BF16)
    return pl.pallas_call(
        _gmlp_out_kernel,
        grid=(n // tm,),
        in_specs=[pl.BlockSpec((tm, e), row), pl.BlockSpec((tm, e), row), _resident((ng, CHUNK, CHUNK)),
                  _resident((CHUNK, ng)), _resident((e, d)), pl.BlockSpec((tm, d), row)],
        out_specs=pl.BlockSpec((tm, d), row),
        out_shape=jax.ShapeDtypeStruct((n, d), F32),
        scratch_shapes=[pltpu.VMEM((tm, e), BF16)],
        compiler_params=_cparams("parallel"),
        name="gmlp_out",
    )(u, v, ws, b_s.T, w_out.astype(BF16), h)


def _ffn_kernel(h_ref, g_ref, wgu_ref, wd_ref, o_ref, *, fc):
    x = h_ref[...]
    hn = _rms(x, g_ref[...]).astype(BF16)
    f = wd_ref.shape[0]
    acc = x
    for f0 in range(0, f, fc):
        gg = _dot(hn, wgu_ref[:, f0:f0 + fc])
        uu = _dot(hn, wgu_ref[:, f + f0:f + f0 + fc])
        a = (gg * jax.nn.sigmoid(gg) * uu).astype(BF16)
        acc = acc + _dot(a, wd_ref[f0:f0 + fc, :])
    o_ref[...] = acc


def _ffn_dense(h, g, w_gu, w_down, *, tm=512):
    n, d = h.shape
    f = w_down.shape[0]
    row = lambda i: (i, 0)
    return pl.pallas_call(
        functools.partial(_ffn_kernel, fc=f // 2),
        grid=(n // tm,),
        in_specs=[pl.BlockSpec((tm, d), row), _resident((1, d)), _resident((d, 2 * f)), _resident((f, d))],
        out_specs=pl.BlockSpec((tm, d), row),
        out_shape=jax.ShapeDtypeStruct((n, d), F32),
        compiler_params=_cparams("parallel"),
        name="ffn_dense",
    )(h, g.reshape(1, d), w_gu.astype(BF16), w_down.astype(BF16))


def _route_kernel(h_ref, g_ref, wr_ref, idx_ref, gate_ref, *, n_experts):
    hn = _rms(h_ref[...], g_ref[...])
    logits = jnp.dot(hn, wr_ref[...], preferred_element_type=F32, precision=lax.Precision.HIGHEST)
    lane = lax.broadcasted_iota(jnp.int32, logits.shape, 1)
    neg = jnp.float32(-jnp.inf)
    lg = jnp.where(lane < n_experts, logits, neg)
    m1 = jnp.max(lg, axis=-1, keepdims=True)
    i1 = jnp.min(jnp.where(lg == m1, lane, LANES), axis=-1, keepdims=True)
    lg2 = jnp.where(lane == i1, neg, lg)
    m2 = jnp.max(lg2, axis=-1, keepdims=True)
    i2 = jnp.min(jnp.where(lg2 == m2, lane, LANES), axis=-1, keepdims=True)
    ex = jnp.exp(m2 - m1)
    g1 = 1.0 / (1.0 + ex)
    g2 = ex / (1.0 + ex)
    idx_ref[...] = jnp.where(lane == 0, i1, jnp.where(lane == 1, i2, 0))
    gate_ref[...] = jnp.where(lane == 0, g1, jnp.where(lane == 1, g2, 0.0))


def _gather_kernel(src_ref, x_hbm, o_ref, sem):
    rows = o_ref.shape[0]

    def row_copy(tok, r):
        return pltpu.make_async_copy(x_hbm.at[pl.ds(tok, 1), :], o_ref.at[pl.ds(r, 1), :], sem)

    def issue(r, c):
        row_copy(src_ref[0, 0, r], r).start()
        return c

    def drain(r, c):
        row_copy(0, r).wait()
        return c

    lax.fori_loop(0, rows, issue, 0, unroll=8)
    lax.fori_loop(0, rows, drain, 0, unroll=8)


def _expert_kernel(te_ref, nu_ref, x_ref, g_ref, wg_ref, wu_ref, wd_ref, o_ref, xn_scr, acc_scr):
    i = pl.program_id(0)
    f = pl.program_id(1)
    last = f == pl.num_programs(1) - 1
    used = i < nu_ref[0]

    @pl.when(jnp.logical_and(used, f == 0))
    def _():
        xn_scr[...] = _rms(x_ref[...], g_ref[...]).astype(BF16)
        acc_scr[...] = jnp.zeros_like(acc_scr)

    @pl.when(used)
    def _():
        xn = xn_scr[...]
        gg = _dot(xn, wg_ref[0])
        uu = _dot(xn, wu_ref[0])
        a = (gg * jax.nn.sigmoid(gg) * uu).astype(BF16)
        acc_scr[...] += _dot(a, wd_ref[0])

    @pl.when(jnp.logical_and(used, last))
    def _():
        o_ref[...] = acc_scr[...]

    @pl.when(jnp.logical_and(jnp.logical_not(used), last))
    def _():
        o_ref[...] = jnp.zeros_like(o_ref)


def _combine_kernel(dst_ref, h_ref, gate_ref, y_hbm, o_ref, buf0, buf1, sem):
    rows = h_ref.shape[0]
    bufs = (buf0, buf1)

    def row_copy(src_row, r, k):
        return pltpu.make_async_copy(y_hbm.at[pl.ds(src_row, 1), :], bufs[k].at[pl.ds(r, 1), :], sem)

    def issue(r, c):
        for k in range(TOP_K):
            row_copy(dst_ref[0, 0, TOP_K * r + k], r, k).start()
        return c

    def drain(r, c):
        for k in range(TOP_K):
            row_copy(0, r, k).wait()
        return c

    lax.fori_loop(0, rows, issue, 0, unroll=8)
    lax.fori_loop(0, rows, drain, 0, unroll=8)
    gate = gate_ref[...]
    o_ref[...] = h_ref[...] + gate[:, 0:1] * buf0[...] + gate[:, 1:2] * buf1[...]


def _moe(h, g, w_router, w_gu, w_down, *, tm=512, tf=1792, rows=256):
    n, d = h.shape
    ne = w_router.shape[1]
    f = w_down.shape[1]
    nf = f // tf
    row = lambda i: (i, 0)
    wr = jnp.zeros((d, LANES), F32).at[:, :ne].set(w_router)
    idx, gate = pl.pallas_call(
        functools.partial(_route_kernel, n_experts=ne),
        grid=(n // tm,),
        in_specs=[pl.BlockSpec((tm, d), row), _resident((1, d)), _resident((d, LANES))],
        out_specs=[pl.BlockSpec((tm, LANES), row), pl.BlockSpec((tm, LANES), row)],
        out_shape=[jax.ShapeDtypeStruct((n, LANES), jnp.int32), jax.ShapeDtypeStruct((n, LANES), F32)],
        compiler_params=_cparams("parallel"),
        name="moe_route",
    )(h, g.reshape(1, d), wr)

    npair = n * TOP_K
    ntile = npair // tm + ne
    e_flat = idx[:, :TOP_K].reshape(npair)
    onehot = (e_flat[:, None] == jnp.arange(ne, dtype=jnp.int32)[None, :]).astype(jnp.int32)
    csum = jnp.cumsum(onehot, axis=0)
    rank = jnp.sum((csum - onehot) * onehot, axis=1)
    counts = csum[-1]
    padded = ((counts + tm - 1) // tm) * tm
    ends = jnp.cumsum(padded)
    dest = (ends - padded)[e_flat] + rank
    src = jnp.zeros((ntile * tm,), jnp.int32).at[dest].set(jnp.arange(npair, dtype=jnp.int32) // TOP_K)
    tile_start = jnp.arange(ntile, dtype=jnp.int32) * tm
    tile_expert = jnp.minimum(jnp.sum((tile_start[:, None] >= ends[None, :]).astype(jnp.int32), axis=1), ne - 1)
    n_used = (ends[-1] // tm).astype(jnp.int32).reshape(1)

    xs = pl.pallas_call(
        _gather_kernel,
        grid=(ntile * tm // rows,),
        in_specs=[pl.BlockSpec((1, 1, rows), lambda i: (i, 0, 0), memory_space=pltpu.SMEM),
                  pl.BlockSpec(memory_space=pl.ANY)],
        out_specs=pl.BlockSpec((rows, d), row),
        out_shape=jax.ShapeDtypeStruct((ntile * tm, d), F32),
        scratch_shapes=[pltpu.SemaphoreType.DMA(())],
        compiler_params=_cparams("arbitrary"),
        name="moe_gather",
    )(src.reshape(-1, 1, rows), h)

    def wsel(off):
        def index(i, j, te, nu):
            return (te[i], 0, off + jnp.where(i < nu[0], j, nf - 1))
        return index

    def wdsel(i, j, te, nu):
        return (te[i], jnp.where(i < nu[0], j, nf - 1), 0)

    ys = pl.pallas_call(
        _expert_kernel,
        grid_spec=pltpu.PrefetchScalarGridSpec(
            num_scalar_prefetch=2,
            grid=(ntile, nf),
            in_specs=[pl.BlockSpec((tm, d), lambda i, j, te, nu: (i, 0)),
                      pl.BlockSpec((1, d), lambda i, j, te, nu: (0, 0)),
                      pl.BlockSpec((1, d, tf), wsel(0)),
                      pl.BlockSpec((1, d, tf), wsel(nf)),
                      pl.BlockSpec((1, tf, d), wdsel)],
            out_specs=pl.BlockSpec((tm, d), lambda i, j, te, nu: (i, 0)),
            scratch_shapes=[pltpu.VMEM((tm, d), BF16), pltpu.VMEM((tm, d), F32)],
        ),
        out_shape=jax.ShapeDtypeStruct((ntile * tm, d), F32),
        compiler_params=_cparams("arbitrary", "arbitrary"),
        name="moe_expert",
    )(tile_expert, n_used, xs, g.reshape(1, d), w_gu.astype(BF16), w_gu.astype(BF16), w_down.astype(BF16))

    return pl.pallas_call(
        _combine_kernel,
        grid=(n // rows,),
        in_specs=[pl.BlockSpec((1, 1, TOP_K * rows), lambda i: (i, 0, 0), memory_space=pltpu.SMEM),
                  pl.BlockSpec((rows, d), row), pl.BlockSpec((rows, LANES), row),
                  pl.BlockSpec(memory_space=pl.ANY)],
        out_specs=pl.BlockSpec((rows, d), row),
        out_shape=jax.ShapeDtypeStruct((n, d), F32),
        scratch_shapes=[pltpu.VMEM((rows, d), F32), pltpu.VMEM((rows, d), F32), pltpu.SemaphoreType.DMA(())],
        compiler_params=_cparams("arbitrary"),
        name="moe_combine",
    )(dest.reshape(-1, 1, TOP_K * rows), h, gate, ys)


def _norm_kernel(h_ref, g_ref, o_ref):
    o_ref[...] = _rms(h_ref[...], g_ref[...]).astype(o_ref.dtype)


def _rmsnorm(h, g, dtype, *, tm=1024):
    n, d = h.shape
    row = lambda i: (i, 0)
    return pl.pallas_call(
        _norm_kernel,
        grid=(n // tm,),
        in_specs=[pl.BlockSpec((tm, d), row), _resident((1, d))],
        out_specs=pl.BlockSpec((tm, d), row),
        out_shape=jax.ShapeDtypeStruct((n, d), dtype),
        compiler_params=_cparams("parallel"),
        name="rmsnorm",
    )(h, g.reshape(1, d))


def _cpow(ar_dt, ai_dt, e):
    mag = jnp.exp(ar_dt * e)
    return mag * jnp.cos(ai_dt * e), mag * jnp.sin(ai_dt * e)


def _s5_prep_kernel(arc_ref, aic_ref, ldc_ref, arr_ref, air_ref, ldr_ref, brt_ref, bit_ref, crt_ref, cit_ref,
                    m_ref, er_ref, ei_ref, cr_ref, ci_ref, ltr_ref, lti_ref):
    p, tw = crt_ref.shape[1:]
    t = tw // S5_GROUP
    hi = lax.Precision.HIGHEST
    dtc = jnp.exp(ldc_ref[0])
    arc = arc_ref[0] * dtc
    aic = aic_ref[0] * dtc
    tau = (lax.broadcasted_iota(jnp.int32, (p, tw), 1) // S5_GROUP).astype(F32)
    crt = crt_ref[0]
    cit = cit_ref[0]
    l0r, l0i = _cpow(arc, aic, tau)
    r_re = l0r * crt - l0i * cit
    r_im = l0r * cit + l0i * crt
    l1r, l1i = _cpow(arc, aic, tau + 1.0)
    cr_ref[0] = (l1r * crt - l1i * cit).astype(BF16)
    ci_ref[0] = (-(l1r * cit + l1i * crt)).astype(BF16)
    ar = arr_ref[0]
    ai = air_ref[0]
    dtr = jnp.exp(ldr_ref[0])
    lr, li = _cpow(ar * dtr, ai * dtr, 1.0)
    den = ar * ar + ai * ai
    nr = lr - 1.0
    kr = (nr * ar + li * ai) / den
    ki = (li * ar - nr * ai) / den
    brt = brt_ref[0]
    bit = bit_ref[0]
    bbr = kr * brt - ki * bit
    bbi = kr * bit + ki * brt
    ltr, lti = _cpow(ar * dtr, ai * dtr, float(t))
    ltr_ref[0] = ltr
    lti_ref[0] = lti
    srow = (lax.broadcasted_iota(jnp.int32, (tw, p), 0) // S5_GROUP).astype(F32)
    pr, pi = _cpow(ar * dtr, ai * dtr, float(t - 1) - srow)
    bbr_t = jnp.concatenate([bbr] * t, axis=0)
    bbi_t = jnp.concatenate([bbi] * t, axis=0)
    er_ref[0] = (pr * bbr_t - pi * bbi_t).astype(BF16)
    ei_ref[0] = (pr * bbi_t + pi * bbr_t).astype(BF16)
    kw = (jnp.dot(bbr, r_re, preferred_element_type=F32, precision=hi)
          - jnp.dot(bbi, r_im, preferred_element_type=F32, precision=hi))
    kz = jnp.concatenate([jnp.zeros_like(kw), kw], axis=1)
    for s in range(t):
        blk = kz if s == 0 else pltpu.roll(kz, s * S5_GROUP, 1)
        m_ref[0, s * S5_GROUP:(s + 1) * S5_GROUP, :] = blk[:, tw:].astype(BF16)


def _s5_scan_kernel(u_ref, m_ref, er_ref, ei_ref, cr_ref, ci_ref, ltr_ref, lti_ref, y_ref,
                    e_re, e_im, s_re, s_im, *, nb):
    u = u_ref[0]
    e_re[...] = _dot(u, er_ref[0])
    e_im[...] = _dot(u, ei_ref[0])
    ltr = ltr_ref[0]
    lti = lti_ref[0]
    nchunk = u.shape[0] // nb
    cur_r = jnp.zeros((nb, ltr.shape[-1]), F32)
    cur_i = jnp.zeros_like(cur_r)
    for c in range(nchunk):
        r = slice(c * nb, (c + 1) * nb)
        s_re[r, :] = cur_r
        s_im[r, :] = cur_i
        cur_r, cur_i = (ltr * cur_r - lti * cur_i + e_re[r, :],
                        ltr * cur_i + lti * cur_r + e_im[r, :])
    y = _dot(u, m_ref[0])
    y = y + _dot(s_re[...].astype(BF16), cr_ref[0]) + _dot(s_im[...].astype(BF16), ci_ref[0])
    y_ref[0] = y.astype(y_ref.dtype)


def _s5_out_kernel(y_ref, h_ref, g_ref, d_ref, w_ref, o_ref):
    x = h_ref[...]
    d = x.shape[-1]
    hn = _rms(x, g_ref[...])
    z = _gelu(y_ref[...].astype(F32) + d_ref[...] * hn).astype(BF16)
    ab = _dot(z, w_ref[...])
    o_ref[...] = x + ab[:, :d] * jax.nn.sigmoid(ab[:, d:])


def _mixer_s5(h, g, bsz, a_re, a_im, log_dt, b_re, b_im, c_re, c_im, d_skip, w_glu, *, tm=512):
    n, d = h.shape
    seq = n // bsz
    ng, p = a_re.shape
    t = S5_T
    tw = t * S5_GROUP
    nc = n // t
    col = lambda a: a.reshape(ng, p, 1)
    rw = lambda a: a.reshape(ng, 1, p)
    ld = jnp.broadcast_to(log_dt[:, None], (ng, p))
    tile_t = lambda c: jnp.tile(jnp.swapaxes(c, 1, 2), (1, 1, t))
    gsel = lambda *shape: pl.BlockSpec((1,) + shape, lambda i: (i, 0, 0))
    m, er, ei, cr, ci, ltr, lti = pl.pallas_call(
        _s5_prep_kernel,
        grid=(ng,),
        in_specs=[gsel(p, 1)] * 3 + [gsel(1, p)] * 3 + [gsel(S5_GROUP, p)] * 2 + [gsel(p, tw)] * 2,
        out_specs=[gsel(tw, tw), gsel(tw, p), gsel(tw, p), gsel(p, tw), gsel(p, tw), gsel(1, p), gsel(1, p)],
        out_shape=[jax.ShapeDtypeStruct((ng, tw, tw), BF16), jax.ShapeDtypeStruct((ng, tw, p), BF16),
                   jax.ShapeDtypeStruct((ng, tw, p), BF16), jax.ShapeDtypeStruct((ng, p, tw), BF16),
                   jax.ShapeDtypeStruct((ng, p, tw), BF16), jax.ShapeDtypeStruct((ng, 1, p), F32),
                   jax.ShapeDtypeStruct((ng, 1, p), F32)],
        compiler_params=_cparams("parallel"),
        name="s5_prep",
    )(col(a_re), col(a_im), col(ld), rw(a_re), rw(a_im), rw(ld),
      jnp.swapaxes(b_re, 1, 2), jnp.swapaxes(b_im, 1, 2), tile_t(c_re), tile_t(c_im))

    hn = _rmsnorm(h, g, BF16)
    ut = hn.reshape(bsz, seq // t, t, ng, S5_GROUP).transpose(3, 1, 0, 2, 4).reshape(ng, nc, tw)
    yt = pl.pallas_call(
        functools.partial(_s5_scan_kernel, nb=bsz),
        grid=(ng,),
        in_specs=[gsel(nc, tw), gsel(tw, tw), gsel(tw, p), gsel(tw, p), gsel(p, tw), gsel(p, tw),
                  gsel(1, p), gsel(1, p)],
        out_specs=gsel(nc, tw),
        out_shape=jax.ShapeDtypeStruct((ng, nc, tw), BF16),
        scratch_shapes=[pltpu.VMEM((nc, p), F32)] * 4,
        compiler_params=_cparams("parallel"),
        name="s5_scan",
    )(ut, m, er, ei, cr, ci, ltr, lti)
    y = yt.reshape(ng, seq // t, bsz, t, S5_GROUP).transpose(2, 1, 3, 0, 4).reshape(n, d)
    row = lambda i: (i, 0)
    return pl.pallas_call(
        _s5_out_kernel,
        grid=(n // tm,),
        in_specs=[pl.BlockSpec((tm, d), row), pl.BlockSpec((tm, d), row), _resident((1, d)), _resident((1, d)),
                  _resident((d, 2 * d))],
        out_specs=pl.BlockSpec((tm, d), row),
        out_shape=jax.ShapeDtypeStruct((n, d), F32),
        compiler_params=_cparams("parallel"),
        name="s5_out",
    )(y, h, g.reshape(1, d), d_skip.reshape(1, d), w_glu.astype(BF16))


def _rg_in_kernel(h_ref, g_ref, w_ref, x_ref, gate_ref):
    r = x_ref.shape[-1]
    hn = _rms(h_ref[...], g_ref[...]).astype(BF16)
    x_ref[...] = _dot(hn, w_ref[:, :r])
    gate_ref[...] = _gelu(_dot(hn, w_ref[:, r:])).astype(BF16)


def _rg_scan_kernel(x_ref, cw_ref, cb_ref, wr_ref, br_ref, wi_ref, bi_ref, lam_ref, o_ref,
                    hist, state, a_scr, b_scr):
    tt, nb, r = x_ref.shape
    kw = cw_ref.shape[0]
    nh = wr_ref.shape[0]
    hd = r // nh

    @pl.when(pl.program_id(0) == 0)
    def _():
        hist[...] = jnp.zeros_like(hist)
        state[...] = jnp.zeros_like(state)

    x = x_ref[...]
    xpad = jnp.concatenate([hist[...], x], axis=0)
    hist[...] = x[tt - (kw - 1):]
    xc = cb_ref[...].reshape(1, 1, r)
    for k in range(kw):
        xc = xc + cw_ref[k:k + 1, :].reshape(1, 1, r) * xpad[k:k + tt]
    xc2 = xc.reshape(tt * nb, r)
    lam = -lam_ref[...]
    sp = jnp.maximum(lam, 0.0) + jnp.log1p(jnp.exp(-jnp.abs(lam)))
    for hh in range(nh):
        cs = slice(hh * hd, (hh + 1) * hd)
        xh = xc2[:, cs]
        xb = xh.astype(BF16)
        rg = jax.nn.sigmoid(_dot(xb, wr_ref[hh]) + br_ref[:, cs])
        ig = jax.nn.sigmoid(_dot(xb, wi_ref[hh]) + bi_ref[:, cs])
        log_a = (-LRU_C) * rg * sp[:, cs]
        a_scr[:, :, cs] = jnp.exp(log_a).reshape(tt, nb, hd)
        b_scr[:, :, cs] = (jnp.sqrt(1.0 - jnp.exp(2.0 * log_a)) * (ig * xh)).reshape(tt, nb, hd)

    def step(t, hcur):
        hnew = a_scr[t] * hcur + b_scr[t]
        b_scr[t] = hnew
        return hnew

    state[...] = lax.fori_loop(0, tt, step, state[...], unroll=8)
    o_ref[...] = b_scr[...].reshape(tt * nb, r).astype(o_ref.dtype)


def _rg_out_kernel(hs_ref, gate_ref, w_ref, h_ref, o_ref):
    y = (hs_ref[...].astype(F32) * gate_ref[...].astype(F32)).astype(BF16)
    o_ref[...] = h_ref[...] + _dot(y, w_ref[...])


def _mixer_rglru(h, g, bsz, w_in, conv_w, conv_b, w_r, b_r, w_i, b_i, lam, w_out, *, tm=512, tt=64):
    n, d = h.shape
    seq = n // bsz
    r = w_out.shape[0]
    nh, hd, _ = w_r.shape
    kw = conv_w.shape[0]
    row = lambda i: (i, 0)
    xb, gate = pl.pallas_call(
        _rg_in_kernel,
        grid=(n // tm,),
        in_specs=[pl.BlockSpec((tm, d), row), _resident((1, d)), _resident((d, 2 * r))],
        out_specs=[pl.BlockSpec((tm, r), row), pl.BlockSpec((tm, r), row)],
        out_shape=[jax.ShapeDtypeStruct((n, r), F32), jax.ShapeDtypeStruct((n, r), BF16)],
        compiler_params=_cparams("parallel"),
        name="rg_in",
    )(h, g.reshape(1, d), w_in.astype(BF16))
    xt = xb.reshape(bsz, seq, r).transpose(1, 0, 2)
    hs_t = pl.pallas_call(
        _rg_scan_kernel,
        grid=(seq // tt,),
        in_specs=[pl.BlockSpec((tt, bsz, r), lambda i: (i, 0, 0)), _resident((kw, r)), _resident((1, r)),
                  _resident((nh, hd, hd)), _resident((1, r)), _resident((nh, hd, hd)), _resident((1, r)),
                  _resident((1, r))],
        out_specs=pl.BlockSpec((tt * bsz, r), row),
        out_shape=jax.ShapeDtypeStruct((seq * bsz, r), BF16),
        scratch_shapes=[pltpu.VMEM((kw - 1, bsz, r), F32), pltpu.VMEM((bsz, r), F32),
                        pltpu.VMEM((tt, bsz, r), F32), pltpu.VMEM((tt, bsz, r), F32)],
        compiler_params=_cparams("arbitrary"),
        name="rg_scan",
    )(xt, conv_w, conv_b.reshape(1, r), w_r.astype(BF16), b_r.reshape(1, r), w_i.astype(BF16),
      b_i.reshape(1, r), lam.reshape(1, r))
    hs = hs_t.reshape(seq, bsz, r).transpose(1, 0, 2).reshape(n, r)
    return pl.pallas_call(
        _rg_out_kernel,
        grid=(n // tm,),
        in_specs=[pl.BlockSpec((tm, r), row), pl.BlockSpec((tm, r), row), _resident((r, d)),
                  pl.BlockSpec((tm, d), row)],
        out_specs=pl.BlockSpec((tm, d), row),
        out_shape=jax.ShapeDtypeStruct((n, d), F32),
        compiler_params=_cparams("parallel"),
        name="rg_out",
    )(hs, gate, w_out.astype(BF16), h)


def kernel(x, norm_mix_g, norm_ffn_g, norm_final_g, a_w_in, a_ln_g, a_ln_b, a_w_s, a_b_s, a_w_out, b_a_re, b_a_im, b_log_dt, b_b_re, b_b_im, b_c_re, b_c_im, b_d, b_w_glu, c_w_in, c_conv_w, c_conv_b, c_w_r, c_b_r, c_w_i, c_b_i, c_lam, c_w_out, f_w_gu, f_w_down, m_w_router, m_w_gu, m_w_down):
    bsz, seq, d = x.shape
    depth = norm_mix_g.shape[0]
    h = x.reshape(bsz * seq, d)
    for i in range(depth):
        kind = i % N_MIXERS
        j = i // N_MIXERS
        if kind == 0:
            h = _mixer_gmlp(h, norm_mix_g[i], a_w_in[j], a_ln_g[j], a_ln_b[j], a_w_s[j], a_b_s[j], a_w_out[j])
        elif kind == 1:
            h = _mixer_s5(h, norm_mix_g[i], bsz, b_a_re[j], b_a_im[j], b_log_dt[j], b_b_re[j], b_b_im[j],
                          b_c_re[j], b_c_im[j], b_d[j], b_w_glu[j])
        else:
            h = _mixer_rglru(h, norm_mix_g[i], bsz, c_w_in[j], c_conv_w[j], c_conv_b[j], c_w_r[j], c_b_r[j],
                             c_w_i[j], c_b_i[j], c_lam[j], c_w_out[j])
        k = i // 2
        if i % 2 == 0:
            h = _ffn_dense(h, norm_ffn_g[i], f_w_gu[k], f_w_down[k])
        else:
            h = _moe(h, norm_ffn_g[i], m_w_router[k], m_w_gu[k], m_w_down[k])
    return _rmsnorm(h, norm_final_g, x.dtype).reshape(bsz, seq, d)
```

```python
import functools
import math

import jax
import jax.numpy as jnp
from jax import lax
from jax.experimental import pallas as pl
from jax.experimental.pallas import tpu as pltpu

F32 = jnp.float32
BF16 = jnp.bfloat16
EPS = 1e-6
N_MIXERS = 3
CHUNK = 128
N_GROUPS_A = 8
S5_GROUP = 16
S5_T = 32
N_HEADS_C = 12
LRU_C = 8.0
TOP_K = 2
LANES = 128
VMEM_LIMIT = 56 * 2**20


def _cparams(*sem):
    return pltpu.CompilerParams(dimension_semantics=sem, vmem_limit_bytes=VMEM_LIMIT)


def _resident(shape):
    nd = len(shape)
    return pl.BlockSpec(shape, lambda *_: (0,) * nd, pipeline_mode=pl.Buffered(1))


def _rms(x, g):
    return x * lax.rsqrt(jnp.mean(x * x, axis=-1, keepdims=True) + EPS) * g


def _gelu(x):
    c = math.sqrt(2.0 / math.pi)
    return x * (0.5 * (1.0 + jnp.tanh(c * (x + 0.044715 * (x * x * x)))))


def _dot(a, b):
    return jnp.dot(a, b, preferred_element_type=F32)


def _gmlp_in_kernel(h_ref, g_ref, w_ref, lng_ref, lnb_ref, u_ref, v_ref):
    e = u_ref.shape[-1]
    hn = _rms(h_ref[...], g_ref[...]).astype(BF16)
    u_ref[...] = _gelu(_dot(hn, w_ref[:, :e])).astype(BF16)
    v = _gelu(_dot(hn, w_ref[:, e:]))
    vc = v - jnp.mean(v, axis=-1, keepdims=True)
    var = jnp.mean(vc * vc, axis=-1, keepdims=True)
    v_ref[...] = (vc * lax.rsqrt(var + EPS) * lng_ref[...] + lnb_ref[...]).astype(BF16)


def _gmlp_out_kernel(u_ref, v_ref, ws_ref, bs_ref, wo_ref, h_ref, o_ref, y_scr):
    tc, e = u_ref.shape
    ng = ws_ref.shape[0]
    ge = e // ng
    for c in range(tc // CHUNK):
        r = slice(c * CHUNK, (c + 1) * CHUNK)
        for g in range(ng):
            cs = slice(g * ge, (g + 1) * ge)
            sv = _dot(ws_ref[g], v_ref[r, cs]) + bs_ref[:, g:g + 1]
            y_scr[r, cs] = (u_ref[r, cs].astype(F32) * sv).astype(BF16)
    o_ref[...] = h_ref[...] + _dot(y_scr[...], wo_ref[...])


def _mixer_gmlp(h, g, w_in, ln_g, ln_b, w_s, b_s, w_out, *, tm=512):
    n, d = h.shape
    e = w_out.shape[0]
    ng = w_s.shape[0]
    row = lambda i: (i, 0)
    u, v = pl.pallas_call(
        _gmlp_in_kernel,
        grid=(n // tm,),
        in_specs=[pl.BlockSpec((tm, d), row), _resident((1, d)), _resident((d, 2 * e)),
                  _resident((1, e)), _resident((1, e))],
        out_specs=[pl.BlockSpec((tm, e), row), pl.BlockSpec((tm, e), row)],
        out_shape=[jax.ShapeDtypeStruct((n, e), BF16)] * 2,
        compiler_params=_cparams("parallel"),
        name="gmlp_in",
    )(h, g.reshape(1, d), w_in.astype(BF16), ln_g.reshape(1, e), ln_b.reshape(1, e))
    mask = jnp.tril(jnp.ones((CHUNK, CHUNK), dtype=bool))
    ws = jnp.where(mask[None], w_s, 0.0).astype(BF16)
    return pl.pallas_call(
        _gmlp_out_kernel,
        grid=(n // tm,),
        in_specs=[pl.BlockSpec((tm, e), row), pl.BlockSpec((tm, e), row), _resident((ng, CHUNK, CHUNK)),
                  _resident((CHUNK, ng)), _resident((e, d)), pl.BlockSpec((tm, d), row)],
        out_specs=pl.BlockSpec((tm, d), row),
        out_shape=jax.ShapeDtypeStruct((n, d), F32),
        scratch_shapes=[pltpu.VMEM((tm, e), BF16)],
        compiler_params=_cparams("parallel"),
        name="gmlp_out",
    )(u, v, ws, b_s.T, w_out.astype(BF16), h)


def _ffn_kernel(h_ref, g_ref, wgu_ref, wd_ref, o_ref, *, fc):
    x = h_ref[...]
    hn = _rms(x, g_ref[...]).astype(BF16)
    f = wd_ref.shape[0]
    acc = x
    for f0 in range(0, f, fc):
        gg = _dot(hn, wgu_ref[:, f0:f0 + fc])
        uu = _dot(hn, wgu_ref[:, f + f0:f + f0 + fc])
        a = (gg * jax.nn.sigmoid(gg) * uu).astype(BF16)
        acc = acc + _dot(a, wd_ref[f0:f0 + fc, :])
    o_ref[...] = acc


def _ffn_dense(h, g, w_gu, w_down, *, tm=512):
    n, d = h.shape
    f = w_down.shape[0]
    row = lambda i: (i, 0)
    return pl.pallas_call(
        functools.partial(_ffn_kernel, fc=f // 2),
        grid=(n // tm,),
        in_specs=[pl.BlockSpec((tm, d), row), _resident((1, d)), _resident((d, 2 * f)), _resident((f, d))],
        out_specs=pl.BlockSpec((tm, d), row),
        out_shape=jax.ShapeDtypeStruct((n, d), F32),
        compiler_params=_cparams("parallel"),
        name="ffn_dense",
    )(h, g.reshape(1, d), w_gu.astype(BF16), w_down.astype(BF16))


def _route_kernel(h_ref, g_ref, wr_ref, idx_ref, gate_ref, *, n_experts):
    hn = _rms(h_ref[...], g_ref[...])
    logits = jnp.dot(hn, wr_ref[...], preferred_element_type=F32, precision=lax.Precision.HIGHEST)
    lane = lax.broadcasted_iota(jnp.int32, logits.shape, 1)
    neg = jnp.float32(-jnp.inf)
    lg = jnp.where(lane < n_experts, logits, neg)
    m1 = jnp.max(lg, axis=-1, keepdims=True)
    i1 = jnp.min(jnp.where(lg == m1, lane, LANES), axis=-1, keepdims=True)
    lg2 = jnp.where(lane == i1, neg, lg)
    m2 = jnp.max(lg2, axis=-1, keepdims=True)
    i2 = jnp.min(jnp.where(lg2 == m2, lane, LANES), axis=-1, keepdims=True)
    ex = jnp.exp(m2 - m1)
    g1 = 1.0 / (1.0 + ex)
    g2 = ex / (1.0 + ex)
    idx_ref[...] = jnp.where(lane == 0, i1, jnp.where(lane == 1, i2, 0))
    gate_ref[...] = jnp.where(lane == 0, g1, jnp.where(lane == 1, g2, 0.0))


def _dispatch_kernel(dst_ref, pad_ref, x_hbm, o_hbm, zbuf, sems, zsem, *, tm):
    i = pl.program_id(0)
    nstep = pl.num_programs(0)
    rows = dst_ref.shape[-1] // TOP_K
    zrows = zbuf.shape[0]

    @pl.when(i == 0)
    def _():
        zbuf[...] = jnp.zeros_like(zbuf)
        for start_wait in (True, False):
            for e in range(pad_ref.shape[0]):
                first = pad_ref[e]

                @pl.when(first >= 0)
                def _():
                    base = pl.multiple_of(jnp.maximum(first, 0), zrows)
                    for q in range(tm // zrows):
                        cp = pltpu.make_async_copy(zbuf, o_hbm.at[pl.ds(base + q * zrows, zrows), :], zsem)
                        cp.start() if start_wait else cp.wait()

    def row_copy(tok, slot_row, sem):
        return pltpu.make_async_copy(x_hbm.at[pl.ds(tok, 1), :], o_hbm.at[pl.ds(slot_row, 1), :], sem)

    def issue(sem):
        def body(r, c):
            for k in range(TOP_K):
                row_copy(i * rows + r, dst_ref[0, 0, TOP_K * r + k], sem).start()
            return c
        lax.fori_loop(0, rows, body, 0, unroll=8)

    def drain(sem):
        def body(r, c):
            for k in range(TOP_K):
                row_copy(0, 0, sem).wait()
            return c
        lax.fori_loop(0, rows, body, 0, unroll=8)

    for par in range(2):
        @pl.when(i % 2 == par)
        def _():
            issue(sems.at[par])

            @pl.when(i > 0)
            def _():
                drain(sems.at[1 - par])

            @pl.when(i == nstep - 1)
            def _():
                drain(sems.at[par])


def _expert_kernel(te_ref, nu_ref, x_ref, g_ref, wg_ref, wu_ref, wd_ref, o_ref, xn_scr):
    i = pl.program_id(0)
    f = pl.program_id(1)
    used = i < nu_ref[0]

    @pl.when(f == 0)
    def _():
        o_ref[...] = jnp.zeros_like(o_ref)

    @pl.when(jnp.logical_and(used, f == 0))
    def _():
        xn_scr[...] = _rms(x_ref[...], g_ref[...]).astype(BF16)

    @pl.when(used)
    def _():
        xn = xn_scr[...]
        gg = _dot(xn, wg_ref[0, 0].astype(BF16))
        uu = _dot(xn, wu_ref[0, 0].astype(BF16))
        a = (gg * jax.nn.sigmoid(gg) * uu).astype(BF16)
        o_ref[...] += _dot(a, wd_ref[0, 0].astype(BF16))


def _combine_kernel(cur_ref, nxt_ref, h_ref, gate_ref, y_hbm, o_ref, buf, sems):
    i = pl.program_id(0)
    nstep = pl.num_programs(0)
    rows = h_ref.shape[0]

    def row_copy(src_row, par, k, r):
        return pltpu.make_async_copy(y_hbm.at[pl.ds(src_row, 1), :], buf.at[par, k, pl.ds(r, 1), :], sems.at[par])

    def issue(idx_ref, par):
        def body(r, c):
            for k in range(TOP_K):
                row_copy(idx_ref[0, 0, TOP_K * r + k], par, k, r).start()
            return c
        lax.fori_loop(0, rows, body, 0, unroll=8)

    def drain(par):
        def body(r, c):
            for k in range(TOP_K):
                row_copy(0, par, k, r).wait()
            return c
        lax.fori_loop(0, rows, body, 0, unroll=8)

    @pl.when(i == 0)
    def _():
        issue(cur_ref, 0)

    for par in range(2):
        @pl.when(i % 2 == par)
        def _():
            @pl.when(i + 1 < nstep)
            def _():
                issue(nxt_ref, 1 - par)

            drain(par)
            gate = gate_ref[...]
            o_ref[...] = h_ref[...] + gate[:, 0:1] * buf[par, 0] + gate[:, 1:2] * buf[par, 1]


def _moe(h, g, layer, w_router, w_gu, w_down, *, tm=1024, tf=896, rows=256, zrows=256):
    n, d = h.shape
    ne = w_router.shape[1]
    f = w_down.shape[2]
    nf = f // tf
    row = lambda i: (i, 0)
    wr = jnp.zeros((d, LANES), F32).at[:, :ne].set(w_router)
    idx, gate = pl.pallas_call(
        functools.partial(_route_kernel, n_experts=ne),
        grid=(n // 512,),
        in_specs=[pl.BlockSpec((512, d), row), _resident((1, d)), _resident((d, LANES))],
        out_specs=[pl.BlockSpec((512, LANES), row), pl.BlockSpec((512, LANES), row)],
        out_shape=[jax.ShapeDtypeStruct((n, LANES), jnp.int32), jax.ShapeDtypeStruct((n, LANES), F32)],
        compiler_params=_cparams("parallel"),
        name="moe_route",
    )(h, g.reshape(1, d), wr)

    npair = n * TOP_K
    ntile = npair // tm + ne
    e_flat = idx[:, :TOP_K].reshape(npair)
    onehot = (e_flat[:, None] == jnp.arange(ne, dtype=jnp.int32)[None, :]).astype(jnp.int32)
    csum = jnp.cumsum(onehot, axis=0)
    rank = jnp.sum((csum - onehot) * onehot, axis=1)
    counts = csum[-1]
    padded = ((counts + tm - 1) // tm) * tm
    ends = jnp.cumsum(padded)
    dest = ((ends - padded)[e_flat] + rank).reshape(n // rows, 1, TOP_K * rows)
    tile_start = jnp.arange(ntile, dtype=jnp.int32) * tm
    spare = ends[-1] + tile_start[:ne]
    pad_tile = jnp.concatenate([jnp.where(padded > 0, ends - tm, -1),
                                jnp.where(spare < ntile * tm, spare, -1)]).astype(jnp.int32)
    tile_expert = jnp.minimum(jnp.sum((tile_start[:, None] >= ends[None, :]).astype(jnp.int32), axis=1), ne - 1)
    n_used = (ends[-1] // tm).astype(jnp.int32).reshape(1)

    nblk = n // rows
    pair_blk = pl.BlockSpec((1, 1, TOP_K * rows), lambda i: (i, 0, 0), memory_space=pltpu.SMEM)
    xs = pl.pallas_call(
        functools.partial(_dispatch_kernel, tm=tm),
        grid=(nblk,),
        in_specs=[pair_blk, pl.BlockSpec(memory_space=pltpu.SMEM), pl.BlockSpec(memory_space=pl.ANY)],
        out_specs=pl.BlockSpec(memory_space=pl.ANY),
        out_shape=jax.ShapeDtypeStruct((ntile * tm, d), F32),
        scratch_shapes=[pltpu.VMEM((zrows, d), F32), pltpu.SemaphoreType.DMA((2,)), pltpu.SemaphoreType.DMA(())],
        compiler_params=_cparams("arbitrary"),
        name="moe_dispatch",
    )(dest, pad_tile, h)

    def wsel(off):
        def index(i, j, te, nu):
            return (layer, te[i], 0, off + jnp.where(i < nu[0], j, nf - 1))
        return index

    def wdsel(i, j, te, nu):
        return (layer, te[i], jnp.where(i < nu[0], j, nf - 1), 0)

    ys = pl.pallas_call(
        _expert_kernel,
        grid_spec=pltpu.PrefetchScalarGridSpec(
            num_scalar_prefetch=2,
            grid=(ntile, nf),
            in_specs=[pl.BlockSpec((tm, d), lambda i, j, te, nu: (jnp.where(i < nu[0], i, 0), 0)),
                      pl.BlockSpec((1, d), lambda i, j, te, nu: (0, 0)),
                      pl.BlockSpec((1, 1, d, tf), wsel(0)),
                      pl.BlockSpec((1, 1, d, tf), wsel(nf)),
                      pl.BlockSpec((1, 1, tf, d), wdsel)],
            out_specs=pl.BlockSpec((tm, d), lambda i, j, te, nu: (i, 0)),
            scratch_shapes=[pltpu.VMEM((tm, d), BF16)],
        ),
        out_shape=jax.ShapeDtypeStruct((ntile * tm, d), F32),
        compiler_params=_cparams("arbitrary", "arbitrary"),
        name="moe_expert",
    )(tile_expert, n_used, xs, g.reshape(1, d), w_gu, w_gu, w_down)

    return pl.pallas_call(
        _combine_kernel,
        grid=(nblk,),
        in_specs=[pair_blk,
                  pl.BlockSpec((1, 1, TOP_K * rows), lambda i: (jnp.minimum(i + 1, nblk - 1), 0, 0),
                               memory_space=pltpu.SMEM),
                  pl.BlockSpec((rows, d), row), pl.BlockSpec((rows, LANES), row),
                  pl.BlockSpec(memory_space=pl.ANY)],
        out_specs=pl.BlockSpec((rows, d), row),
        out_shape=jax.ShapeDtypeStruct((n, d), F32),
        scratch_shapes=[pltpu.VMEM((2, TOP_K, rows, d), F32), pltpu.SemaphoreType.DMA((2,))],
        compiler_params=_cparams("arbitrary"),
        name="moe_combine",
    )(dest, dest, h, gate, ys)


def _norm_kernel(h_ref, g_ref, o_ref):
    o_ref[...] = _rms(h_ref[...], g_ref[...]).astype(o_ref.dtype)


def _rmsnorm(h, g, dtype, *, tm=1024):
    n, d = h.shape
    row = lambda i: (i, 0)
    return pl.pallas_call(
        _norm_kernel,
        grid=(n // tm,),
        in_specs=[pl.BlockSpec((tm, d), row), _resident((1, d))],
        out_specs=pl.BlockSpec((tm, d), row),
        out_shape=jax.ShapeDtypeStruct((n, d), dtype),
        compiler_params=_cparams("parallel"),
        name="rmsnorm",
    )(h, g.reshape(1, d))


def _cpow(ar_dt, ai_dt, e):
    mag = jnp.exp(ar_dt * e)
    return mag * jnp.cos(ai_dt * e), mag * jnp.sin(ai_dt * e)


def _s5_prep_kernel(arc_ref, aic_ref, ldc_ref, arr_ref, air_ref, ldr_ref, brt_ref, bit_ref, crt_ref, cit_ref,
                    m_ref, er_ref, ei_ref, cr_ref, ci_ref, ltr_ref, lti_ref):
    p, tw = crt_ref.shape[1:]
    t = tw // S5_GROUP
    hi = lax.Precision.HIGHEST
    dtc = jnp.exp(ldc_ref[0])
    arc = arc_ref[0] * dtc
    aic = aic_ref[0] * dtc
    tau = lax.broadcasted_iota(jnp.int32, (p, LANES), 1).astype(F32)
    lpr, lpi = _cpow(arc, aic, tau)
    sel = lax.broadcasted_iota(jnp.int32, (LANES, tw), 0)
    lane_tau = lax.broadcasted_iota(jnp.int32, (LANES, tw), 1) // S5_GROUP
    ex0 = (sel == lane_tau).astype(F32)
    ex1 = (sel == lane_tau + 1).astype(F32)
    expand = lambda a, ex: jnp.dot(a, ex, preferred_element_type=F32, precision=hi)
    crt = crt_ref[0]
    cit = cit_ref[0]
    l0r, l0i = expand(lpr, ex0), expand(lpi, ex0)
    r_re = l0r * crt - l0i * cit
    r_im = l0r * cit + l0i * crt
    l1r, l1i = expand(lpr, ex1), expand(lpi, ex1)
    cr_ref[0] = (l1r * crt - l1i * cit).astype(BF16)
    ci_ref[0] = (-(l1r * cit + l1i * crt)).astype(BF16)
    ar = arr_ref[0]
    ai = air_ref[0]
    dtr = jnp.exp(ldr_ref[0])
    lr, li = _cpow(ar * dtr, ai * dtr, 1.0)
    den = ar * ar + ai * ai
    nr = lr - 1.0
    kr = (nr * ar + li * ai) / den
    ki = (li * ar - nr * ai) / den
    brt = brt_ref[0]
    bit = bit_ref[0]
    bbr = kr * brt - ki * bit
    bbi = kr * bit + ki * brt
    ltr, lti = _cpow(ar * dtr, ai * dtr, float(t))
    ltr_ref[0] = ltr
    lti_ref[0] = lti
    srow = lax.broadcasted_iota(jnp.int32, (t, p), 0).astype(F32)
    pr, pi = _cpow(ar * dtr, ai * dtr, float(t - 1) - srow)
    rep = lambda a: jnp.broadcast_to(a[:, None, :], (t, S5_GROUP, p)).reshape(tw, p)
    pr, pi = rep(pr), rep(pi)
    bbr_t = jnp.concatenate([bbr] * t, axis=0)
    bbi_t = jnp.concatenate([bbi] * t, axis=0)
    er_ref[0] = (pr * bbr_t - pi * bbi_t).astype(BF16)
    ei_ref[0] = (pr * bbi_t + pi * bbr_t).astype(BF16)
    kw = (jnp.dot(bbr, r_re, preferred_element_type=F32, precision=hi)
          - jnp.dot(bbi, r_im, preferred_element_type=F32, precision=hi))
    kz = jnp.concatenate([jnp.zeros_like(kw), kw], axis=1)
    for s in range(t):
        blk = kz if s == 0 else pltpu.roll(kz, s * S5_GROUP, 1)
        m_ref[0, s * S5_GROUP:(s + 1) * S5_GROUP, :] = blk[:, tw:].astype(BF16)


def _s5_scan_kernel(u_ref, m_ref, er_ref, ei_ref, cr_ref, ci_ref, ltr_ref, lti_ref, y_ref,
                    e_re, e_im, s_re, s_im, *, nb):
    u = u_ref[0]
    e_re[...] = _dot(u, er_ref[0])
    e_im[...] = _dot(u, ei_ref[0])
    ltr = ltr_ref[0]
    lti = lti_ref[0]
    nchunk = u.shape[0] // nb
    cur_r = jnp.zeros((nb, ltr.shape[-1]), F32)
    cur_i = jnp.zeros_like(cur_r)
    for c in range(nchunk):
        r = slice(c * nb, (c + 1) * nb)
        s_re[r, :] = cur_r
        s_im[r, :] = cur_i
        cur_r, cur_i = (ltr * cur_r - lti * cur_i + e_re[r, :],
                        ltr * cur_i + lti * cur_r + e_im[r, :])
    y = _dot(u, m_ref[0])
    y = y + _dot(s_re[...].astype(BF16), cr_ref[0]) + _dot(s_im[...].astype(BF16), ci_ref[0])
    y_ref[0] = y.astype(y_ref.dtype)


def _s5_out_kernel(y_ref, h_ref, g_ref, d_ref, w_ref, o_ref):
    x = h_ref[...]
    d = x.shape[-1]
    hn = _rms(x, g_ref[...])
    z = _gelu(y_ref[...].astype(F32) + d_ref[...] * hn).astype(BF16)
    ab = _dot(z, w_ref[...])
    o_ref[...] = x + ab[:, :d] * jax.nn.sigmoid(ab[:, d:])


def _mixer_s5(h, g, bsz, a_re, a_im, log_dt, b_re, b_im, c_re, c_im, d_skip, w_glu, *, tm=512):
    n, d = h.shape
    seq = n // bsz
    ng, p = a_re.shape
    t = S5_T
    tw = t * S5_GROUP
    nc = n // t
    col = lambda a: a.reshape(ng, p, 1)
    rw = lambda a: a.reshape(ng, 1, p)
    ld = jnp.broadcast_to(log_dt[:, None], (ng, p))
    tile_t = lambda c: jnp.tile(jnp.swapaxes(c, 1, 2), (1, 1, t))
    gsel = lambda *shape: pl.BlockSpec((1,) + shape, lambda i: (i, 0, 0))
    m, er, ei, cr, ci, ltr, lti = pl.pallas_call(
        _s5_prep_kernel,
        grid=(ng,),
        in_specs=[gsel(p, 1)] * 3 + [gsel(1, p)] * 3 + [gsel(S5_GROUP, p)] * 2 + [gsel(p, tw)] * 2,
        out_specs=[gsel(tw, tw), gsel(tw, p), gsel(tw, p), gsel(p, tw), gsel(p, tw), gsel(1, p), gsel(1, p)],
        out_shape=[jax.ShapeDtypeStruct((ng, tw, tw), BF16), jax.ShapeDtypeStruct((ng, tw, p), BF16),
                   jax.ShapeDtypeStruct((ng, tw, p), BF16), jax.ShapeDtypeStruct((ng, p, tw), BF16),
                   jax.ShapeDtypeStruct((ng, p, tw), BF16), jax.ShapeDtypeStruct((ng, 1, p), F32),
                   jax.ShapeDtypeStruct((ng, 1, p), F32)],
        compiler_params=_cparams("parallel"),
        name="s5_prep",
    )(col(a_re), col(a_im), col(ld), rw(a_re), rw(a_im), rw(ld),
      jnp.swapaxes(b_re, 1, 2), jnp.swapaxes(b_im, 1, 2), tile_t(c_re), tile_t(c_im))

    hn = _rmsnorm(h, g, BF16)
    ut = hn.reshape(bsz, seq // t, t, ng, S5_GROUP).transpose(3, 1, 0, 2, 4).reshape(ng, nc, tw)
    yt = pl.pallas_call(
        functools.partial(_s5_scan_kernel, nb=bsz),
        grid=(ng,),
        in_specs=[gsel(nc, tw), gsel(tw, tw), gsel(tw, p), gsel(tw, p), gsel(p, tw), gsel(p, tw),
                  gsel(1, p), gsel(1, p)],
        out_specs=gsel(nc, tw),
        out_shape=jax.ShapeDtypeStruct((ng, nc, tw), BF16),
        scratch_shapes=[pltpu.VMEM((nc, p), F32)] * 4,
        compiler_params=_cparams("parallel"),
        name="s5_scan",
    )(ut, m, er, ei, cr, ci, ltr, lti)
    y = yt.reshape(ng, seq // t, bsz, t, S5_GROUP).transpose(2, 1, 3, 0, 4).reshape(n, d)
    row = lambda i: (i, 0)
    return pl.pallas_call(
        _s5_out_kernel,
        grid=(n // tm,),
        in_specs=[pl.BlockSpec((tm, d), row), pl.BlockSpec((tm, d), row), _resident((1, d)), _resident((1, d)),
                  _resident((d, 2 * d))],
        out_specs=pl.BlockSpec((tm, d), row),
        out_shape=jax.ShapeDtypeStruct((n, d), F32),
        compiler_params=_cparams("parallel"),
        name="s5_out",
    )(y, h, g.reshape(1, d), d_skip.reshape(1, d), w_glu.astype(BF16))


def _rg_in_kernel(h_ref, g_ref, w_ref, x_ref, gate_ref):
    r = x_ref.shape[-1]
    hn = _rms(h_ref[...], g_ref[...]).astype(BF16)
    x_ref[...] = _dot(hn, w_ref[:, :r])
    gate_ref[...] = _gelu(_dot(hn, w_ref[:, r:])).astype(BF16)


def _rg_scan_kernel(x_ref, cw_ref, cb_ref, wr_ref, br_ref, wi_ref, bi_ref, lam_ref, o_ref,
                    hist, state, a_scr, b_scr):
    tt, nb, r = x_ref.shape
    kw = cw_ref.shape[0]
    nh = wr_ref.shape[0]
    hd = r // nh

    @pl.when(pl.program_id(0) == 0)
    def _():
        hist[...] = jnp.zeros_like(hist)
        state[...] = jnp.zeros_like(state)

    x = x_ref[...]
    xpad = jnp.concatenate([hist[...], x], axis=0)
    hist[...] = x[tt - (kw - 1):]
    xc = cb_ref[...].reshape(1, 1, r)
    for k in range(kw):
        xc = xc + cw_ref[k:k + 1, :].reshape(1, 1, r) * xpad[k:k + tt]
    xc2 = xc.reshape(tt * nb, r)
    lam = -lam_ref[...]
    sp = jnp.maximum(lam, 0.0) + jnp.log1p(jnp.exp(-jnp.abs(lam)))
    for hh in range(nh):
        cs = slice(hh * hd, (hh + 1) * hd)
        xh = xc2[:, cs]
        xb = xh.astype(BF16)
        rg = jax.nn.sigmoid(_dot(xb, wr_ref[hh]) + br_ref[:, cs])
        ig = jax.nn.sigmoid(_dot(xb, wi_ref[hh]) + bi_ref[:, cs])
        log_a = (-LRU_C) * rg * sp[:, cs]
        a_scr[:, :, cs] = jnp.exp(log_a).reshape(tt, nb, hd)
        b_scr[:, :, cs] = (jnp.sqrt(1.0 - jnp.exp(2.0 * log_a)) * (ig * xh)).reshape(tt, nb, hd)

    def step(t, hcur):
        hnew = a_scr[t] * hcur + b_scr[t]
        b_scr[t] = hnew
        return hnew

    state[...] = lax.fori_loop(0, tt, step, state[...], unroll=8)
    o_ref[...] = b_scr[...].reshape(tt * nb, r).astype(o_ref.dtype)


def _rg_out_kernel(hs_ref, gate_ref, w_ref, h_ref, o_ref):
    y = (hs_ref[...].astype(F32) * gate_ref[...].astype(F32)).astype(BF16)
    o_ref[...] = h_ref[...] + _dot(y, w_ref[...])


def _mixer_rglru(h, g, bsz, w_in, conv_w, conv_b, w_r, b_r, w_i, b_i, lam, w_out, *, tm=512, tt=64):
    n, d = h.shape
    seq = n // bsz
    r = w_out.shape[0]
    nh, hd, _ = w_r.shape
    kw = conv_w.shape[0]
    row = lambda i: (i, 0)
    xb, gate = pl.pallas_call(
        _rg_in_kernel,
        grid=(n // tm,),
        in_specs=[pl.BlockSpec((tm, d), row), _resident((1, d)), _resident((d, 2 * r))],
        out_specs=[pl.BlockSpec((tm, r), row), pl.BlockSpec((tm, r), row)],
        out_shape=[jax.ShapeDtypeStruct((n, r), F32), jax.ShapeDtypeStruct((n, r), BF16)],
        compiler_params=_cparams("parallel"),
        name="rg_in",
    )(h, g.reshape(1, d), w_in.astype(BF16))
    xt = xb.reshape(bsz, seq, r).transpose(1, 0, 2)
    hs_t = pl.pallas_call(
        _rg_scan_kernel,
        grid=(seq // tt,),
        in_specs=[pl.BlockSpec((tt, bsz, r), lambda i: (i, 0, 0)), _resident((kw, r)), _resident((1, r)),
                  _resident((nh, hd, hd)), _resident((1, r)), _resident((nh, hd, hd)), _resident((1, r)),
                  _resident((1, r))],
        out_specs=pl.BlockSpec((tt * bsz, r), row),
        out_shape=jax.ShapeDtypeStruct((seq * bsz, r), BF16),
        scratch_shapes=[pltpu.VMEM((kw - 1, bsz, r), F32), pltpu.VMEM((bsz, r), F32),
                        pltpu.VMEM((tt, bsz, r), F32), pltpu.VMEM((tt, bsz, r), F32)],
        compiler_params=_cparams("arbitrary"),
        name="rg_scan",
    )(xt, conv_w, conv_b.reshape(1, r), w_r.astype(BF16), b_r.reshape(1, r), w_i.astype(BF16),
      b_i.reshape(1, r), lam.reshape(1, r))
    hs = hs_t.reshape(seq, bsz, r).transpose(1, 0, 2).reshape(n, r)
    return pl.pallas_call(
        _rg_out_kernel,
        grid=(n // tm,),
        in_specs=[pl.BlockSpec((tm, r), row), pl.BlockSpec((tm, r), row), _resident((r, d)),
                  pl.BlockSpec((tm, d), row)],
        out_specs=pl.BlockSpec((tm, d), row),
        out_shape=jax.ShapeDtypeStruct((n, d), F32),
        compiler_params=_cparams("parallel"),
        name="rg_out",
    )(hs, gate, w_out.astype(BF16), h)


def kernel(x, norm_mix_g, norm_ffn_g, norm_final_g, a_w_in, a_ln_g, a_ln_b, a_w_s, a_b_s, a_w_out, b_a_re, b_a_im, b_log_dt, b_b_re, b_b_im, b_c_re, b_c_im, b_d, b_w_glu, c_w_in, c_conv_w, c_conv_b, c_w_r, c_b_r, c_w_i, c_b_i, c_lam, c_w_out, f_w_gu, f_w_down, m_w_router, m_w_gu, m_w_down):
    bsz, seq, d = x.shape
    depth = norm_mix_g.shape[0]
    h = x.reshape(bsz * seq, d)
    for i in range(depth):
        kind = i % N_MIXERS
        j = i // N_MIXERS
        if kind == 0:
            h = _mixer_gmlp(h, norm_mix_g[i], a_w_in[j], a_ln_g[j], a_ln_b[j], a_w_s[j], a_b_s[j], a_w_out[j])
        elif kind == 1:
            h = _mixer_s5(h, norm_mix_g[i], bsz, b_a_re[j], b_a_im[j], b_log_dt[j], b_b_re[j], b_b_im[j],
                          b_c_re[j], b_c_im[j], b_d[j], b_w_glu[j])
        else:
            h = _mixer_rglru(h, norm_mix_g[i], bsz, c_w_in[j], c_conv_w[j], c_conv_b[j], c_w_r[j], c_b_r[j],
                             c_w_i[j], c_b_i[j], c_lam[j], c_w_out[j])
        k = i // 2
        if i % 2 == 0:
            h = _ffn_dense(h, norm_ffn_g[i], f_w_gu[k], f_w_down[k])
        else:
            h = _moe(h, norm_ffn_g[i], k, m_w_router[k], m_w_gu, m_w_down)
    return _rmsnorm(h, norm_final_g, x.dtype).reshape(bsz, seq, d)
```

```python
import functools
import math

import jax
import jax.numpy as jnp
from jax import lax
from jax.experimental import pallas as pl
from jax.experimental.pallas import tpu as pltpu

F32 = jnp.float32
BF16 = jnp.bfloat16
EPS = 1e-6
N_MIXERS = 3
CHUNK = 128
N_GROUPS_A = 8
S5_GROUP = 16
S5_T = 32
N_HEADS_C = 12
LRU_C = 8.0
TOP_K = 2
LANES = 128
VMEM_LIMIT = 56 * 2**20


def _cparams(*sem):
    return pltpu.CompilerParams(dimension_semantics=sem, vmem_limit_bytes=VMEM_LIMIT)


def _resident(shape):
    nd = len(shape)
    return pl.BlockSpec(shape, lambda *_: (0,) * nd, pipeline_mode=pl.Buffered(1))


def _rms(x, g):
    return x * lax.rsqrt(jnp.mean(x * x, axis=-1, keepdims=True) + EPS) * g


def _gelu(x):
    c = math.sqrt(2.0 / math.pi)
    return x * (0.5 * (1.0 + jnp.tanh(c * (x + 0.044715 * (x * x * x)))))


def _dot(a, b):
    return jnp.dot(a, b, preferred_element_type=F32)


def _gmlp_in_kernel(h_ref, g_ref, w_ref, lng_ref, lnb_ref, u_ref, v_ref):
    e = u_ref.shape[-1]
    hn = _rms(h_ref[...], g_ref[...]).astype(BF16)
    u_ref[...] = _gelu(_dot(hn, w_ref[:, :e])).astype(BF16)
    v = _gelu(_dot(hn, w_ref[:, e:]))
    vc = v - jnp.mean(v, axis=-1, keepdims=True)
    var = jnp.mean(vc * vc, axis=-1, keepdims=True)
    v_ref[...] = (vc * lax.rsqrt(var + EPS) * lng_ref[...] + lnb_ref[...]).astype(BF16)


def _gmlp_out_kernel(u_ref, v_ref, ws_ref, bs_ref, wo_ref, h_ref, o_ref, y_scr):
    tc, e = u_ref.shape
    ng = ws_ref.shape[0]
    ge = e // ng
    for c in range(tc // CHUNK):
        r = slice(c * CHUNK, (c + 1) * CHUNK)
        for g in range(ng):
            cs = slice(g * ge, (g + 1) * ge)
            sv = _dot(ws_ref[g], v_ref[r, cs]) + bs_ref[:, g:g + 1]
            y_scr[r, cs] = (u_ref[r, cs].astype(F32) * sv).astype(BF16)
    o_ref[...] = h_ref[...] + _dot(y_scr[...], wo_ref[...])


def _mixer_gmlp(h, g, w_in, ln_g, ln_b, w_s, b_s, w_out, *, tm=512):
    n, d = h.shape
    e = w_out.shape[0]
    ng = w_s.shape[0]
    row = lambda i: (i, 0)
    u, v = pl.pallas_call(
        _gmlp_in_kernel,
        grid=(n // tm,),
        in_specs=[pl.BlockSpec((tm, d), row), _resident((1, d)), _resident((d, 2 * e)),
                  _resident((1, e)), _resident((1, e))],
        out_specs=[pl.BlockSpec((tm, e), row), pl.BlockSpec((tm, e), row)],
        out_shape=[jax.ShapeDtypeStruct((n, e), BF16)] * 2,
        compiler_params=_cparams("parallel"),
        name="gmlp_in",
    )(h, g.reshape(1, d), w_in.astype(BF16), ln_g.reshape(1, e), ln_b.reshape(1, e))
    mask = jnp.tril(jnp.ones((CHUNK, CHUNK), dtype=bool))
    ws = jnp.where(mask[None], w_s, 0.0).astype(BF16)
    return pl.pallas_call(
        _gmlp_out_kernel,
        grid=(n // tm,),
        in_specs=[pl.BlockSpec((tm, e), row), pl.BlockSpec((tm, e), row), _resident((ng, CHUNK, CHUNK)),
                  _resident((CHUNK, ng)), _resident((e, d)), pl.BlockSpec((tm, d), row)],
        out_specs=pl.BlockSpec((tm, d), row),
        out_shape=jax.ShapeDtypeStruct((n, d), F32),
        scratch_shapes=[pltpu.VMEM((tm, e), BF16)],
        compiler_params=_cparams("parallel"),
        name="gmlp_out",
    )(u, v, ws, b_s.T, w_out.astype(BF16), h)


def _ffn_kernel(h_ref, g_ref, wgu_ref, wd_ref, o_ref, *, fc):
    x = h_ref[...]
    hn = _rms(x, g_ref[...]).astype(BF16)
    f = wd_ref.shape[0]
    acc = x
    for f0 in range(0, f, fc):
        gg = _dot(hn, wgu_ref[:, f0:f0 + fc])
        uu = _dot(hn, wgu_ref[:, f + f0:f + f0 + fc])
        a = (gg * jax.nn.sigmoid(gg) * uu).astype(BF16)
        acc = acc + _dot(a, wd_ref[f0:f0 + fc, :])
    o_ref[...] = acc


def _ffn_dense(h, g, w_gu, w_down, *, tm=512):
    n, d = h.shape
    f = w_down.shape[0]
    row = lambda i: (i, 0)
    return pl.pallas_call(
        functools.partial(_ffn_kernel, fc=f // 2),
        grid=(n // tm,),
        in_specs=[pl.BlockSpec((tm, d), row), _resident((1, d)), _resident((d, 2 * f)), _resident((f, d))],
        out_specs=pl.BlockSpec((tm, d), row),
        out_shape=jax.ShapeDtypeStruct((n, d), F32),
        compiler_params=_cparams("parallel"),
        name="ffn_dense",
    )(h, g.reshape(1, d), w_gu.astype(BF16), w_down.astype(BF16))


def _route_kernel(h_ref, g_ref, wr_ref, idx_ref, gate_ref, *, n_experts):
    hn = _rms(h_ref[...], g_ref[...])
    logits = jnp.dot(hn, wr_ref[...], preferred_element_type=F32, precision=lax.Precision.HIGHEST)
    lane = lax.broadcasted_iota(jnp.int32, logits.shape, 1)
    neg = jnp.float32(-jnp.inf)
    lg = jnp.where(lane < n_experts, logits, neg)
    m1 = jnp.max(lg, axis=-1, keepdims=True)
    i1 = jnp.min(jnp.where(lg == m1, lane, LANES), axis=-1, keepdims=True)
    lg2 = jnp.where(lane == i1, neg, lg)
    m2 = jnp.max(lg2, axis=-1, keepdims=True)
    i2 = jnp.min(jnp.where(lg2 == m2, lane, LANES), axis=-1, keepdims=True)
    ex = jnp.exp(m2 - m1)
    g1 = 1.0 / (1.0 + ex)
    g2 = ex / (1.0 + ex)
    idx_ref[...] = jnp.where(lane == 0, i1, jnp.where(lane == 1, i2, 0))
    gate_ref[...] = jnp.where(lane == 0, g1, jnp.where(lane == 1, g2, 0.0))


def _dispatch_kernel(dst_ref, pad_ref, x_ref, o_hbm, zbuf, sem, zsem, *, tm):
    i = pl.program_id(0)
    rows = x_ref.shape[0]
    zrows = zbuf.shape[0]

    @pl.when(i == 0)
    def _():
        zbuf[...] = jnp.zeros_like(zbuf)
        for start_wait in (True, False):
            for e in range(pad_ref.shape[0]):
                first = pad_ref[e]

                @pl.when(first >= 0)
                def _():
                    base = pl.multiple_of(jnp.maximum(first, 0), zrows)
                    for q in range(tm // zrows):
                        cp = pltpu.make_async_copy(zbuf, o_hbm.at[pl.ds(base + q * zrows, zrows), :], zsem)
                        cp.start() if start_wait else cp.wait()

    def row_copy(r, slot_row):
        return pltpu.make_async_copy(x_ref.at[pl.ds(r, 1), :], o_hbm.at[pl.ds(slot_row, 1), :], sem)

    def issue(r, c):
        for k in range(TOP_K):
            row_copy(r, dst_ref[0, 0, TOP_K * r + k]).start()
        return c

    def drain(r, c):
        for k in range(TOP_K):
            row_copy(r, 0).wait()
        return c

    lax.fori_loop(0, rows, issue, 0, unroll=8)
    lax.fori_loop(0, rows, drain, 0, unroll=8)


def _expert_kernel(te_ref, nu_ref, x_ref, g_ref, wg_ref, wu_ref, wd_ref, o_ref, xn_scr):
    i = pl.program_id(0)
    f = pl.program_id(1)
    used = i < nu_ref[0]

    @pl.when(f == 0)
    def _():
        o_ref[...] = jnp.zeros_like(o_ref)

    @pl.when(jnp.logical_and(used, f == 0))
    def _():
        xn_scr[...] = _rms(x_ref[...], g_ref[...]).astype(BF16)

    @pl.when(used)
    def _():
        xn = xn_scr[...]
        gg = _dot(xn, wg_ref[0, 0].astype(BF16))
        uu = _dot(xn, wu_ref[0, 0].astype(BF16))
        a = (gg * jax.nn.sigmoid(gg) * uu).astype(BF16)
        o_ref[...] += _dot(a, wd_ref[0, 0].astype(BF16))


def _combine_kernel(cur_ref, nxt_ref, h_ref, gate_ref, y_hbm, o_ref, buf, sems):
    i = pl.program_id(0)
    nstep = pl.num_programs(0)
    rows = h_ref.shape[0]

    def row_copy(src_row, par, k, r):
        return pltpu.make_async_copy(y_hbm.at[pl.ds(src_row, 1), :], buf.at[par, k, pl.ds(r, 1), :], sems.at[par])

    def issue(idx_ref, par):
        def body(r, c):
            for k in range(TOP_K):
                row_copy(idx_ref[0, 0, TOP_K * r + k], par, k, r).start()
            return c
        lax.fori_loop(0, rows, body, 0, unroll=8)

    def drain(par):
        def body(r, c):
            for k in range(TOP_K):
                row_copy(0, par, k, r).wait()
            return c
        lax.fori_loop(0, rows, body, 0, unroll=8)

    @pl.when(i == 0)
    def _():
        issue(cur_ref, 0)

    for par in range(2):
        @pl.when(i % 2 == par)
        def _():
            @pl.when(i + 1 < nstep)
            def _():
                issue(nxt_ref, 1 - par)

            drain(par)
            gate = gate_ref[...]
            o_ref[...] = h_ref[...] + gate[:, 0:1] * buf[par, 0] + gate[:, 1:2] * buf[par, 1]


def _moe(h, g, layer, w_router, w_gu, w_down, *, tm=1024, tf=896, rows=256, drows=512, zrows=256):
    n, d = h.shape
    ne = w_router.shape[1]
    f = w_down.shape[2]
    nf = f // tf
    row = lambda i: (i, 0)
    wr = jnp.zeros((d, LANES), F32).at[:, :ne].set(w_router)
    idx, gate = pl.pallas_call(
        functools.partial(_route_kernel, n_experts=ne),
        grid=(n // 512,),
        in_specs=[pl.BlockSpec((512, d), row), _resident((1, d)), _resident((d, LANES))],
        out_specs=[pl.BlockSpec((512, LANES), row), pl.BlockSpec((512, LANES), row)],
        out_shape=[jax.ShapeDtypeStruct((n, LANES), jnp.int32), jax.ShapeDtypeStruct((n, LANES), F32)],
        compiler_params=_cparams("parallel"),
        name="moe_route",
    )(h, g.reshape(1, d), wr)

    npair = n * TOP_K
    ntile = npair // tm + ne
    e_flat = idx[:, :TOP_K].reshape(npair)
    onehot = (e_flat[:, None] == jnp.arange(ne, dtype=jnp.int32)[None, :]).astype(jnp.int32)
    csum = jnp.cumsum(onehot, axis=0)
    rank = jnp.sum((csum - onehot) * onehot, axis=1)
    counts = csum[-1]
    padded = ((counts + tm - 1) // tm) * tm
    ends = jnp.cumsum(padded)
    dest = (ends - padded)[e_flat] + rank
    tile_start = jnp.arange(ntile, dtype=jnp.int32) * tm
    spare = ends[-1] + tile_start[:ne]
    pad_tile = jnp.concatenate([jnp.where(padded > 0, ends - tm, -1),
                                jnp.where(spare < ntile * tm, spare, -1)]).astype(jnp.int32)
    tile_expert = jnp.minimum(jnp.sum((tile_start[:, None] >= ends[None, :]).astype(jnp.int32), axis=1), ne - 1)
    n_used = (ends[-1] // tm).astype(jnp.int32).reshape(1)

    nblk = n // rows
    pair_blk = lambda r: pl.BlockSpec((1, 1, TOP_K * r), lambda i: (i, 0, 0), memory_space=pltpu.SMEM)
    xs = pl.pallas_call(
        functools.partial(_dispatch_kernel, tm=tm),
        grid=(n // drows,),
        in_specs=[pair_blk(drows), pl.BlockSpec(memory_space=pltpu.SMEM), pl.BlockSpec((drows, d), row)],
        out_specs=pl.BlockSpec(memory_space=pl.ANY),
        out_shape=jax.ShapeDtypeStruct((ntile * tm, d), F32),
        scratch_shapes=[pltpu.VMEM((zrows, d), F32), pltpu.SemaphoreType.DMA(()), pltpu.SemaphoreType.DMA(())],
        compiler_params=_cparams("arbitrary"),
        name="moe_dispatch",
    )(dest.reshape(n // drows, 1, TOP_K * drows), pad_tile, h)
    dest = dest.reshape(nblk, 1, TOP_K * rows)

    def wsel(off):
        def index(i, j, te, nu):
            return (layer, te[i], 0, off + jnp.where(i < nu[0], j, nf - 1))
        return index

    def wdsel(i, j, te, nu):
        return (layer, te[i], jnp.where(i < nu[0], j, nf - 1), 0)

    ys = pl.pallas_call(
        _expert_kernel,
        grid_spec=pltpu.PrefetchScalarGridSpec(
            num_scalar_prefetch=2,
            grid=(ntile, nf),
            in_specs=[pl.BlockSpec((tm, d), lambda i, j, te, nu: (jnp.where(i < nu[0], i, 0), 0)),
                      pl.BlockSpec((1, d), lambda i, j, te, nu: (0, 0)),
                      pl.BlockSpec((1, 1, d, tf), wsel(0)),
                      pl.BlockSpec((1, 1, d, tf), wsel(nf)),
                      pl.BlockSpec((1, 1, tf, d), wdsel)],
            out_specs=pl.BlockSpec((tm, d), lambda i, j, te, nu: (i, 0)),
            scratch_shapes=[pltpu.VMEM((tm, d), BF16)],
        ),
        out_shape=jax.ShapeDtypeStruct((ntile * tm, d), F32),
        compiler_params=_cparams("arbitrary", "arbitrary"),
        name="moe_expert",
    )(tile_expert, n_used, xs, g.reshape(1, d), w_gu, w_gu, w_down)

    return pl.pallas_call(
        _combine_kernel,
        grid=(nblk,),
        in_specs=[pair_blk(rows),
                  pl.BlockSpec((1, 1, TOP_K * rows), lambda i: (jnp.minimum(i + 1, nblk - 1), 0, 0),
                               memory_space=pltpu.SMEM),
                  pl.BlockSpec((rows, d), row), pl.BlockSpec((rows, LANES), row),
                  pl.BlockSpec(memory_space=pl.ANY)],
        out_specs=pl.BlockSpec((rows, d), row),
        out_shape=jax.ShapeDtypeStruct((n, d), F32),
        scratch_shapes=[pltpu.VMEM((2, TOP_K, rows, d), F32), pltpu.SemaphoreType.DMA((2,))],
        compiler_params=_cparams("arbitrary"),
        name="moe_combine",
    )(dest, dest, h, gate, ys)


def _norm_kernel(h_ref, g_ref, o_ref):
    o_ref[...] = _rms(h_ref[...], g_ref[...]).astype(o_ref.dtype)


def _rmsnorm(h, g, dtype, *, tm=1024):
    n, d = h.shape
    row = lambda i: (i, 0)
    return pl.pallas_call(
        _norm_kernel,
        grid=(n // tm,),
        in_specs=[pl.BlockSpec((tm, d), row), _resident((1, d))],
        out_specs=pl.BlockSpec((tm, d), row),
        out_shape=jax.ShapeDtypeStruct((n, d), dtype),
        compiler_params=_cparams("parallel"),
        name="rmsnorm",
    )(h, g.reshape(1, d))


def _cpow(ar_dt, ai_dt, e):
    mag = jnp.exp(ar_dt * e)
    return mag * jnp.cos(ai_dt * e), mag * jnp.sin(ai_dt * e)


def _s5_prep_kernel(arc_ref, aic_ref, ldc_ref, arr_ref, air_ref, ldr_ref, brt_ref, bit_ref, crt_ref, cit_ref,
                    m_ref, er_ref, ei_ref, cr_ref, ci_ref, ltr_ref, lti_ref):
    p, tw = crt_ref.shape[1:]
    t = tw // S5_GROUP
    hi = lax.Precision.HIGHEST
    dtc = jnp.exp(ldc_ref[0])
    arc = arc_ref[0] * dtc
    aic = aic_ref[0] * dtc
    tau = lax.broadcasted_iota(jnp.int32, (p, LANES), 1).astype(F32)
    lpr, lpi = _cpow(arc, aic, tau)
    sel = lax.broadcasted_iota(jnp.int32, (LANES, tw), 0)
    lane_tau = lax.broadcasted_iota(jnp.int32, (LANES, tw), 1) // S5_GROUP
    ex0 = (sel == lane_tau).astype(F32)
    ex1 = (sel == lane_tau + 1).astype(F32)
    expand = lambda a, ex: jnp.dot(a, ex, preferred_element_type=F32, precision=hi)
    crt = crt_ref[0]
    cit = cit_ref[0]
    l0r, l0i = expand(lpr, ex0), expand(lpi, ex0)
    r_re = l0r * crt - l0i * cit
    r_im = l0r * cit + l0i * crt
    l1r, l1i = expand(lpr, ex1), expand(lpi, ex1)
    cr_ref[0] = (l1r * crt - l1i * cit).astype(BF16)
    ci_ref[0] = (-(l1r * cit + l1i * crt)).astype(BF16)
    ar = arr_ref[0]
    ai = air_ref[0]
    dtr = jnp.exp(ldr_ref[0])
    lr, li = _cpow(ar * dtr, ai * dtr, 1.0)
    den = ar * ar + ai * ai
    nr = lr - 1.0
    kr = (nr * ar + li * ai) / den
    ki = (li * ar - nr * ai) / den
    brt = brt_ref[0]
    bit = bit_ref[0]
    bbr = kr * brt - ki * bit
    bbi = kr * bit + ki * brt
    ltr, lti = _cpow(ar * dtr, ai * dtr, float(t))
    ltr_ref[0] = ltr
    lti_ref[0] = lti
    srow = lax.broadcasted_iota(jnp.int32, (t, p), 0).astype(F32)
    pr, pi = _cpow(ar * dtr, ai * dtr, float(t - 1) - srow)
    rep = lambda a: jnp.broadcast_to(a[:, None, :], (t, S5_GROUP, p)).reshape(tw, p)
    pr, pi = rep(pr), rep(pi)
    bbr_t = jnp.concatenate([bbr] * t, axis=0)
    bbi_t = jnp.concatenate([bbi] * t, axis=0)
    er_ref[0] = (pr * bbr_t - pi * bbi_t).astype(BF16)
    ei_ref[0] = (pr * bbi_t + pi * bbr_t).astype(BF16)
    kw = (jnp.dot(bbr, r_re, preferred_element_type=F32, precision=hi)
          - jnp.dot(bbi, r_im, preferred_element_type=F32, precision=hi))
    kz = jnp.concatenate([jnp.zeros_like(kw), kw], axis=1)
    for s in range(t):
        blk = kz if s == 0 else pltpu.roll(kz, s * S5_GROUP, 1)
        m_ref[0, s * S5_GROUP:(s + 1) * S5_GROUP, :] = blk[:, tw:].astype(BF16)


def _s5_scan_kernel(u_ref, m_ref, er_ref, ei_ref, cr_ref, ci_ref, ltr_ref, lti_ref, y_ref,
                    e_re, e_im, s_re, s_im, *, nb):
    u = u_ref[0]
    e_re[...] = _dot(u, er_ref[0])
    e_im[...] = _dot(u, ei_ref[0])
    ltr = ltr_ref[0]
    lti = lti_ref[0]
    nchunk = u.shape[0] // nb
    cur_r = jnp.zeros((nb, ltr.shape[-1]), F32)
    cur_i = jnp.zeros_like(cur_r)
    for c in range(nchunk):
        r = slice(c * nb, (c + 1) * nb)
        s_re[r, :] = cur_r
        s_im[r, :] = cur_i
        cur_r, cur_i = (ltr * cur_r - lti * cur_i + e_re[r, :],
                        ltr * cur_i + lti * cur_r + e_im[r, :])
    y = _dot(u, m_ref[0])
    y = y + _dot(s_re[...].astype(BF16), cr_ref[0]) + _dot(s_im[...].astype(BF16), ci_ref[0])
    y_ref[0] = y.astype(y_ref.dtype)


def _s5_out_kernel(y_ref, h_ref, g_ref, d_ref, w_ref, o_ref):
    x = h_ref[...]
    d = x.shape[-1]
    hn = _rms(x, g_ref[...])
    z = _gelu(y_ref[...].astype(F32) + d_ref[...] * hn).astype(BF16)
    ab = _dot(z, w_ref[...])
    o_ref[...] = x + ab[:, :d] * jax.nn.sigmoid(ab[:, d:])


def _mixer_s5(h, g, bsz, a_re, a_im, log_dt, b_re, b_im, c_re, c_im, d_skip, w_glu, *, tm=512):
    n, d = h.shape
    seq = n // bsz
    ng, p = a_re.shape
    t = S5_T
    tw = t * S5_GROUP
    nc = n // t
    col = lambda a: a.reshape(ng, p, 1)
    rw = lambda a: a.reshape(ng, 1, p)
    ld = jnp.broadcast_to(log_dt[:, None], (ng, p))
    tile_t = lambda c: jnp.tile(jnp.swapaxes(c, 1, 2), (1, 1, t))
    gsel = lambda *shape: pl.BlockSpec((1,) + shape, lambda i: (i, 0, 0))
    m, er, ei, cr, ci, ltr, lti = pl.pallas_call(
        _s5_prep_kernel,
        grid=(ng,),
        in_specs=[gsel(p, 1)] * 3 + [gsel(1, p)] * 3 + [gsel(S5_GROUP, p)] * 2 + [gsel(p, tw)] * 2,
        out_specs=[gsel(tw, tw), gsel(tw, p), gsel(tw, p), gsel(p, tw), gsel(p, tw), gsel(1, p), gsel(1, p)],
        out_shape=[jax.ShapeDtypeStruct((ng, tw, tw), BF16), jax.ShapeDtypeStruct((ng, tw, p), BF16),
                   jax.ShapeDtypeStruct((ng, tw, p), BF16), jax.ShapeDtypeStruct((ng, p, tw), BF16),
                   jax.ShapeDtypeStruct((ng, p, tw), BF16), jax.ShapeDtypeStruct((ng, 1, p), F32),
                   jax.ShapeDtypeStruct((ng, 1, p), F32)],
        compiler_params=_cparams("parallel"),
        name="s5_prep",
    )(col(a_re), col(a_im), col(ld), rw(a_re), rw(a_im), rw(ld),
      jnp.swapaxes(b_re, 1, 2), jnp.swapaxes(b_im, 1, 2), tile_t(c_re), tile_t(c_im))

    hn = _rmsnorm(h, g, BF16)
    ut = hn.reshape(bsz, seq // t, t, ng, S5_GROUP).transpose(3, 1, 0, 2, 4).reshape(ng, nc, tw)
    yt = pl.pallas_call(
        functools.partial(_s5_scan_kernel, nb=bsz),
        grid=(ng,),
        in_specs=[gsel(nc, tw), gsel(tw, tw), gsel(tw, p), gsel(tw, p), gsel(p, tw), gsel(p, tw),
                  gsel(1, p), gsel(1, p)],
        out_specs=gsel(nc, tw),
        out_shape=jax.ShapeDtypeStruct((ng, nc, tw), BF16),
        scratch_shapes=[pltpu.VMEM((nc, p), F32)] * 4,
        compiler_params=_cparams("parallel"),
        name="s5_scan",
    )(ut, m, er, ei, cr, ci, ltr, lti)
    y = yt.reshape(ng, seq // t, bsz, t, S5_GROUP).transpose(2, 1, 3, 0, 4).reshape(n, d)
    row = lambda i: (i, 0)
    return pl.pallas_call(
        _s5_out_kernel,
        grid=(n // tm,),
        in_specs=[pl.BlockSpec((tm, d), row), pl.BlockSpec((tm, d), row), _resident((1, d)), _resident((1, d)),
                  _resident((d, 2 * d))],
        out_specs=pl.BlockSpec((tm, d), row),
        out_shape=jax.ShapeDtypeStruct((n, d), F32),
        compiler_params=_cparams("parallel"),
        name="s5_out",
    )(y, h, g.reshape(1, d), d_skip.reshape(1, d), w_glu.astype(BF16))


def _rg_in_kernel(h_ref, g_ref, w_ref, x_ref, gate_ref):
    r = x_ref.shape[-1]
    hn = _rms(h_ref[...], g_ref[...]).astype(BF16)
    x_ref[...] = _dot(hn, w_ref[:, :r])
    gate_ref[...] = _gelu(_dot(hn, w_ref[:, r:])).astype(BF16)


def _rg_scan_kernel(x_ref, cw_ref, cb_ref, wr_ref, br_ref, wi_ref, bi_ref, lam_ref, o_ref,
                    hist, state, a_scr, b_scr):
    tt, nb, r = x_ref.shape
    kw = cw_ref.shape[0]
    nh = wr_ref.shape[0]
    hd = r // nh

    @pl.when(pl.program_id(0) == 0)
    def _():
        hist[...] = jnp.zeros_like(hist)
        state[...] = jnp.zeros_like(state)

    x = x_ref[...]
    xpad = jnp.concatenate([hist[...], x], axis=0)
    hist[...] = x[tt - (kw - 1):]
    xc = cb_ref[...].reshape(1, 1, r)
    for k in range(kw):
        xc = xc + cw_ref[k:k + 1, :].reshape(1, 1, r) * xpad[k:k + tt]
    xc2 = xc.reshape(tt * nb, r)
    lam = -lam_ref[...]
    sp = jnp.maximum(lam, 0.0) + jnp.log1p(jnp.exp(-jnp.abs(lam)))
    for hh in range(nh):
        cs = slice(hh * hd, (hh + 1) * hd)
        xh = xc2[:, cs]
        xb = xh.astype(BF16)
        rg = jax.nn.sigmoid(_dot(xb, wr_ref[hh]) + br_ref[:, cs])
        ig = jax.nn.sigmoid(_dot(xb, wi_ref[hh]) + bi_ref[:, cs])
        log_a = (-LRU_C) * rg * sp[:, cs]
        a_scr[:, :, cs] = jnp.exp(log_a).reshape(tt, nb, hd)
        b_scr[:, :, cs] = (jnp.sqrt(1.0 - jnp.exp(2.0 * log_a)) * (ig * xh)).reshape(tt, nb, hd)

    def step(t, hcur):
        hnew = a_scr[t] * hcur + b_scr[t]
        b_scr[t] = hnew
        return hnew

    state[...] = lax.fori_loop(0, tt, step, state[...], unroll=8)
    o_ref[...] = b_scr[...].reshape(tt * nb, r).astype(o_ref.dtype)


def _rg_out_kernel(hs_ref, gate_ref, w_ref, h_ref, o_ref):
    y = (hs_ref[...].astype(F32) * gate_ref[...].astype(F32)).astype(BF16)
    o_ref[...] = h_ref[...] + _dot(y, w_ref[...])


def _mixer_rglru(h, g, bsz, w_in, conv_w, conv_b, w_r, b_r, w_i, b_i, lam, w_out, *, tm=512, tt=64):
    n, d = h.shape
    seq = n // bsz
    r = w_out.shape[0]
    nh, hd, _ = w_r.shape
    kw = conv_w.shape[0]
    row = lambda i: (i, 0)
    xb, gate = pl.pallas_call(
        _rg_in_kernel,
        grid=(n // tm,),
        in_specs=[pl.BlockSpec((tm, d), row), _resident((1, d)), _resident((d, 2 * r))],
        out_specs=[pl.BlockSpec((tm, r), row), pl.BlockSpec((tm, r), row)],
        out_shape=[jax.ShapeDtypeStruct((n, r), F32), jax.ShapeDtypeStruct((n, r), BF16)],
        compiler_params=_cparams("parallel"),
        name="rg_in",
    )(h, g.reshape(1, d), w_in.astype(BF16))
    xt = xb.reshape(bsz, seq, r).transpose(1, 0, 2)
    hs_t = pl.pallas_call(
        _rg_scan_kernel,
        grid=(seq // tt,),
        in_specs=[pl.BlockSpec((tt, bsz, r), lambda i: (i, 0, 0)), _resident((kw, r)), _resident((1, r)),
                  _resident((nh, hd, hd)), _resident((1, r)), _resident((nh, hd, hd)), _resident((1, r)),
                  _resident((1, r))],
        out_specs=pl.BlockSpec((tt * bsz, r), row),
        out_shape=jax.ShapeDtypeStruct((seq * bsz, r), BF16),
        scratch_shapes=[pltpu.VMEM((kw - 1, bsz, r), F32), pltpu.VMEM((bsz, r), F32),
                        pltpu.VMEM((tt, bsz, r), F32), pltpu.VMEM((tt, bsz, r), F32)],
        compiler_params=_cparams("arbitrary"),
        name="rg_scan",
    )(xt, conv_w, conv_b.reshape(1, r), w_r.astype(BF16), b_r.reshape(1, r), w_i.astype(BF16),
      b_i.reshape(1, r), lam.reshape(1, r))
    hs = hs_t.reshape(seq, bsz, r).transpose(1, 0, 2).reshape(n, r)
    return pl.pallas_call(
        _rg_out_kernel,
        grid=(n // tm,),
        in_specs=[pl.BlockSpec((tm, r), row), pl.BlockSpec((tm, r), row), _resident((r, d)),
                  pl.BlockSpec((tm, d), row)],
        out_specs=pl.BlockSpec((tm, d), row),
        out_shape=jax.ShapeDtypeStruct((n, d), F32),
        compiler_params=_cparams("parallel"),
        name="rg_out",
    )(hs, gate, w_out.astype(BF16), h)


def kernel(x, norm_mix_g, norm_ffn_g, norm_final_g, a_w_in, a_ln_g, a_ln_b, a_w_s, a_b_s, a_w_out, b_a_re, b_a_im, b_log_dt, b_b_re, b_b_im, b_c_re, b_c_im, b_d, b_w_glu, c_w_in, c_conv_w, c_conv_b, c_w_r, c_b_r, c_w_i, c_b_i, c_lam, c_w_out, f_w_gu, f_w_down, m_w_router, m_w_gu, m_w_down):
    bsz, seq, d = x.shape
    depth = norm_mix_g.shape[0]
    h = x.reshape(bsz * seq, d)
    for i in range(depth):
        kind = i % N_MIXERS
        j = i // N_MIXERS
        if kind == 0:
            h = _mixer_gmlp(h, norm_mix_g[i], a_w_in[j], a_ln_g[j], a_ln_b[j], a_w_s[j], a_b_s[j], a_w_out[j])
        elif kind == 1:
            h = _mixer_s5(h, norm_mix_g[i], bsz, b_a_re[j], b_a_im[j], b_log_dt[j], b_b_re[j], b_b_im[j],
                          b_c_re[j], b_c_im[j], b_d[j], b_w_glu[j])
        else:
            h = _mixer_rglru(h, norm_mix_g[i], bsz, c_w_in[j], c_conv_w[j], c_conv_b[j], c_w_r[j], c_b_r[j],
                             c_w_i[j], c_b_i[j], c_lam[j], c_w_out[j])
        k = i // 2
        if i % 2 == 0:
            h = _ffn_dense(h, norm_ffn_g[i], f_w_gu[k], f_w_down[k])
        else:
            h = _moe(h, norm_ffn_g[i], k, m_w_router[k], m_w_gu, m_w_down)
    return _rmsnorm(h, norm_final_g, x.dtype).reshape(bsz, seq, d)
```

```python
import functools
import math

import jax
import jax.numpy as jnp
from jax import lax
from jax.experimental import pallas as pl
from jax.experimental.pallas import tpu as pltpu

F32 = jnp.float32
BF16 = jnp.bfloat16
EPS = 1e-6
N_MIXERS = 3
CHUNK = 128
N_GROUPS_A = 8
S5_GROUP = 16
N_HEADS_C = 12
LRU_C = 8.0
TOP_K = 2
LANES = 128
VMEM_LIMIT = 56 * 2**20


def _cparams(*sem):
    return pltpu.CompilerParams(dimension_semantics=sem, vmem_limit_bytes=VMEM_LIMIT)


def _resident(shape):
    nd = len(shape)
    return pl.BlockSpec(shape, lambda *_: (0,) * nd, pipeline_mode=pl.Buffered(1))


def _rms(x, g):
    return x * lax.rsqrt(jnp.mean(x * x, axis=-1, keepdims=True) + EPS) * g


def _gelu(x):
    c = math.sqrt(2.0 / math.pi)
    return x * (0.5 * (1.0 + jnp.tanh(c * (x + 0.044715 * (x * x * x)))))


def _dot(a, b):
    return jnp.dot(a, b, preferred_element_type=F32)


def _gmlp_in_kernel(h_ref, g_ref, w_ref, lng_ref, lnb_ref, u_ref, v_ref):
    e = u_ref.shape[-1]
    hn = _rms(h_ref[...], g_ref[...]).astype(BF16)
    u_ref[...] = _gelu(_dot(hn, w_ref[:, :e])).astype(BF16)
    v = _gelu(_dot(hn, w_ref[:, e:]))
    vc = v - jnp.mean(v, axis=-1, keepdims=True)
    var = jnp.mean(vc * vc, axis=-1, keepdims=True)
    v_ref[...] = (vc * lax.rsqrt(var + EPS) * lng_ref[...] + lnb_ref[...]).astype(BF16)


def _gmlp_out_kernel(u_ref, v_ref, ws_ref, bs_ref, wo_ref, h_ref, o_ref, y_scr):
    tc, e = u_ref.shape
    ng = ws_ref.shape[0]
    ge = e // ng
    for c in range(tc // CHUNK):
        r = slice(c * CHUNK, (c + 1) * CHUNK)
        for g in range(ng):
            cs = slice(g * ge, (g + 1) * ge)
            sv = _dot(ws_ref[g], v_ref[r, cs]) + bs_ref[:, g:g + 1]
            y_scr[r, cs] = (u_ref[r, cs].astype(F32) * sv).astype(BF16)
    o_ref[...] = h_ref[...] + _dot(y_scr[...], wo_ref[...])


def _mixer_gmlp(h, g, w_in, ln_g, ln_b, w_s, b_s, w_out, *, tm=512):
    n, d = h.shape
    e = w_out.shape[0]
    ng = w_s.shape[0]
    row = lambda i: (i, 0)
    u, v = pl.pallas_call(
        _gmlp_in_kernel,
        grid=(n // tm,),
        in_specs=[pl.BlockSpec((tm, d), row), _resident((1, d)), _resident((d, 2 * e)),
                  _resident((1, e)), _resident((1, e))],
        out_specs=[pl.BlockSpec((tm, e), row), pl.BlockSpec((tm, e), row)],
        out_shape=[jax.ShapeDtypeStruct((n, e), BF16)] * 2,
        compiler_params=_cparams("parallel"),
        name="gmlp_in",
    )(h, g.reshape(1, d), w_in.astype(BF16), ln_g.reshape(1, e), ln_b.reshape(1, e))
    mask = jnp.tril(jnp.ones((CHUNK, CHUNK), dtype=bool))
    ws = jnp.where(mask[None], w_s, 0.0).astype(BF16)
    return pl.pallas_call(
        _gmlp_out_kernel,
        grid=(n // tm,),
        in_specs=[pl.BlockSpec((tm, e), row), pl.BlockSpec((tm, e), row), _resident((ng, CHUNK, CHUNK)),
                  _resident((CHUNK, ng)), _resident((e, d)), pl.BlockSpec((tm, d), row)],
        out_specs=pl.BlockSpec((tm, d), row),
        out_shape=jax.ShapeDtypeStruct((n, d), F32),
        scratch_shapes=[pltpu.VMEM((tm, e), BF16)],
        compiler_params=_cparams("parallel"),
        name="gmlp_out",
    )(u, v, ws, b_s.T, w_out.astype(BF16), h)


def _ffn_kernel(h_ref, g_ref, wgu_ref, wd_ref, o_ref, *, fc):
    x = h_ref[...]
    hn = _rms(x, g_ref[...]).astype(BF16)
    f = wd_ref.shape[0]
    acc = x
    for f0 in range(0, f, fc):
        gg = _dot(hn, wgu_ref[:, f0:f0 + fc])
        uu = _dot(hn, wgu_ref[:, f + f0:f + f0 + fc])
        a = (gg * jax.nn.sigmoid(gg) * uu).astype(BF16)
        acc = acc + _dot(a, wd_ref[f0:f0 + fc, :])
    o_ref[...] = acc


def _ffn_dense(h, g, w_gu, w_down, *, tm=512):
    n, d = h.shape
    f = w_down.shape[0]
    row = lambda i: (i, 0)
    return pl.pallas_call(
        functools.partial(_ffn_kernel, fc=f // 2),
        grid=(n // tm,),
        in_specs=[pl.BlockSpec((tm, d), row), _resident((1, d)), _resident((d, 2 * f)), _resident((f, d))],
        out_specs=pl.BlockSpec((tm, d), row),
        out_shape=jax.ShapeDtypeStruct((n, d), F32),
        compiler_params=_cparams("parallel"),
        name="ffn_dense",
    )(h, g.reshape(1, d), w_gu.astype(BF16), w_down.astype(BF16))


def _route_kernel(h_ref, g_ref, wr_ref, idx_ref, gate_ref, *, n_experts):
    hn = _rms(h_ref[...], g_ref[...])
    logits = jnp.dot(hn, wr_ref[...], preferred_element_type=F32, precision=lax.Precision.HIGHEST)
    lane = lax.broadcasted_iota(jnp.int32, logits.shape, 1)
    neg = jnp.float32(-jnp.inf)
    lg = jnp.where(lane < n_experts, logits, neg)
    m1 = jnp.max(lg, axis=-1, keepdims=True)
    i1 = jnp.min(jnp.where(lg == m1, lane, LANES), axis=-1, keepdims=True)
    lg2 = jnp.where(lane == i1, neg, lg)
    m2 = jnp.max(lg2, axis=-1, keepdims=True)
    i2 = jnp.min(jnp.where(lg2 == m2, lane, LANES), axis=-1, keepdims=True)
    ex = jnp.exp(m2 - m1)
    g1 = 1.0 / (1.0 + ex)
    g2 = ex / (1.0 + ex)
    idx_ref[...] = jnp.where(lane == 0, i1, jnp.where(lane == 1, i2, 0))
    gate_ref[...] = jnp.where(lane == 0, g1, jnp.where(lane == 1, g2, 0.0))


def _dispatch_kernel(dst_ref, pad_ref, x_ref, o_hbm, zbuf, sem, zsem, *, tm):
    i = pl.program_id(0)
    rows = x_ref.shape[0]
    zrows = zbuf.shape[0]

    @pl.when(i == 0)
    def _():
        zbuf[...] = jnp.zeros_like(zbuf)
        for start_wait in (True, False):
            for e in range(pad_ref.shape[0]):
                first = pad_ref[e]

                @pl.when(first >= 0)
                def _():
                    base = pl.multiple_of(jnp.maximum(first, 0), zrows)
                    for q in range(tm // zrows):
                        cp = pltpu.make_async_copy(zbuf, o_hbm.at[pl.ds(base + q * zrows, zrows), :], zsem)
                        cp.start() if start_wait else cp.wait()

    def row_copy(r, slot_row):
        return pltpu.make_async_copy(x_ref.at[pl.ds(r, 1), :], o_hbm.at[pl.ds(slot_row, 1), :], sem)

    def issue(r, c):
        for k in range(TOP_K):
            row_copy(r, dst_ref[0, 0, TOP_K * r + k]).start()
        return c

    def drain(r, c):
        for k in range(TOP_K):
            row_copy(r, 0).wait()
        return c

    lax.fori_loop(0, rows, issue, 0, unroll=8)
    lax.fori_loop(0, rows, drain, 0, unroll=8)


def _expert_kernel(te_ref, nu_ref, x_ref, g_ref, wg_ref, wu_ref, wd_ref, o_ref, xn_scr):
    i = pl.program_id(0)
    f = pl.program_id(1)
    used = i < nu_ref[0]

    @pl.when(f == 0)
    def _():
        o_ref[...] = jnp.zeros_like(o_ref)

    @pl.when(jnp.logical_and(used, f == 0))
    def _():
        xn_scr[...] = _rms(x_ref[...], g_ref[...]).astype(BF16)

    @pl.when(used)
    def _():
        xn = xn_scr[...]
        gg = _dot(xn, wg_ref[0, 0].astype(BF16))
        uu = _dot(xn, wu_ref[0, 0].astype(BF16))
        a = (gg * jax.nn.sigmoid(gg) * uu).astype(BF16)
        o_ref[...] += _dot(a, wd_ref[0, 0].astype(BF16))


def _combine_kernel(cur_ref, nxt_ref, h_ref, gate_ref, y_hbm, o_ref, buf, sems):
    i = pl.program_id(0)
    nstep = pl.num_programs(0)
    rows = h_ref.shape[0]

    def row_copy(src_row, par, k, r):
        return pltpu.make_async_copy(y_hbm.at[pl.ds(src_row, 1), :], buf.at[par, k, pl.ds(r, 1), :], sems.at[par])

    def issue(idx_ref, par):
        def body(r, c):
            for k in range(TOP_K):
                row_copy(idx_ref[0, 0, TOP_K * r + k], par, k, r).start()
            return c
        lax.fori_loop(0, rows, body, 0, unroll=8)

    def drain(par):
        def body(r, c):
            for k in range(TOP_K):
                row_copy(0, par, k, r).wait()
            return c
        lax.fori_loop(0, rows, body, 0, unroll=8)

    @pl.when(i == 0)
    def _():
        issue(cur_ref, 0)

    for par in range(2):
        @pl.when(i % 2 == par)
        def _():
            @pl.when(i + 1 < nstep)
            def _():
                issue(nxt_ref, 1 - par)

            drain(par)
            gate = gate_ref[...]
            o_ref[...] = h_ref[...] + gate[:, 0:1] * buf[par, 0] + gate[:, 1:2] * buf[par, 1]


def _moe(h, g, layer, w_router, w_gu, w_down, *, tm=1024, tf=896, rows=256, drows=512, zrows=256):
    n, d = h.shape
    ne = w_router.shape[1]
    f = w_down.shape[2]
    nf = f // tf
    row = lambda i: (i, 0)
    wr = jnp.zeros((d, LANES), F32).at[:, :ne].set(w_router)
    idx, gate = pl.pallas_call(
        functools.partial(_route_kernel, n_experts=ne),
        grid=(n // 512,),
        in_specs=[pl.BlockSpec((512, d), row), _resident((1, d)), _resident((d, LANES))],
        out_specs=[pl.BlockSpec((512, LANES), row), pl.BlockSpec((512, LANES), row)],
        out_shape=[jax.ShapeDtypeStruct((n, LANES), jnp.int32), jax.ShapeDtypeStruct((n, LANES), F32)],
        compiler_params=_cparams("parallel"),
        name="moe_route",
    )(h, g.reshape(1, d), wr)

    npair = n * TOP_K
    ntile = npair // tm + ne
    e_flat = idx[:, :TOP_K].reshape(npair)
    onehot = (e_flat[:, None] == jnp.arange(ne, dtype=jnp.int32)[None, :]).astype(jnp.int32)
    csum = jnp.cumsum(onehot, axis=0)
    rank = jnp.sum((csum - onehot) * onehot, axis=1)
    counts = csum[-1]
    padded = ((counts + tm - 1) // tm) * tm
    ends = jnp.cumsum(padded)
    dest = (ends - padded)[e_flat] + rank
    tile_start = jnp.arange(ntile, dtype=jnp.int32) * tm
    spare = ends[-1] + tile_start[:ne]
    pad_tile = jnp.concatenate([jnp.where(padded > 0, ends - tm, -1),
                                jnp.where(spare < ntile * tm, spare, -1)]).astype(jnp.int32)
    tile_expert = jnp.minimum(jnp.sum((tile_start[:, None] >= ends[None, :]).astype(jnp.int32), axis=1), ne - 1)
    n_used = (ends[-1] // tm).astype(jnp.int32).reshape(1)

    nblk = n // rows
    pair_blk = lambda r: pl.BlockSpec((1, 1, TOP_K * r), lambda i: (i, 0, 0), memory_space=pltpu.SMEM)
    xs = pl.pallas_call(
        functools.partial(_dispatch_kernel, tm=tm),
        grid=(n // drows,),
        in_specs=[pair_blk(drows), pl.BlockSpec(memory_space=pltpu.SMEM), pl.BlockSpec((drows, d), row)],
        out_specs=pl.BlockSpec(memory_space=pl.ANY),
        out_shape=jax.ShapeDtypeStruct((ntile * tm, d), F32),
        scratch_shapes=[pltpu.VMEM((zrows, d), F32), pltpu.SemaphoreType.DMA(()), pltpu.SemaphoreType.DMA(())],
        compiler_params=_cparams("arbitrary"),
        name="moe_dispatch",
    )(dest.reshape(n // drows, 1, TOP_K * drows), pad_tile, h)
    dest = dest.reshape(nblk, 1, TOP_K * rows)

    def wsel(off):
        def index(i, j, te, nu):
            return (layer, te[i], 0, off + jnp.where(i < nu[0], j, nf - 1))
        return index

    def wdsel(i, j, te, nu):
        return (layer, te[i], jnp.where(i < nu[0], j, nf - 1), 0)

    ys = pl.pallas_call(
        _expert_kernel,
        grid_spec=pltpu.PrefetchScalarGridSpec(
            num_scalar_prefetch=2,
            grid=(ntile, nf),
            in_specs=[pl.BlockSpec((tm, d), lambda i, j, te, nu: (jnp.where(i < nu[0], i, 0), 0)),
                      pl.BlockSpec((1, d), lambda i, j, te, nu: (0, 0)),
                      pl.BlockSpec((1, 1, d, tf), wsel(0)),
                      pl.BlockSpec((1, 1, d, tf), wsel(nf)),
                      pl.BlockSpec((1, 1, tf, d), wdsel)],
            out_specs=pl.BlockSpec((tm, d), lambda i, j, te, nu: (i, 0)),
            scratch_shapes=[pltpu.VMEM((tm, d), BF16)],
        ),
        out_shape=jax.ShapeDtypeStruct((ntile * tm, d), F32),
        compiler_params=_cparams("arbitrary", "arbitrary"),
        name="moe_expert",
    )(tile_expert, n_used, xs, g.reshape(1, d), w_gu, w_gu, w_down)

    return pl.pallas_call(
        _combine_kernel,
        grid=(nblk,),
        in_specs=[pair_blk(rows),
                  pl.BlockSpec((1, 1, TOP_K * rows), lambda i: (jnp.minimum(i + 1, nblk - 1), 0, 0),
                               memory_space=pltpu.SMEM),
                  pl.BlockSpec((rows, d), row), pl.BlockSpec((rows, LANES), row),
                  pl.BlockSpec(memory_space=pl.ANY)],
        out_specs=pl.BlockSpec((rows, d), row),
        out_shape=jax.ShapeDtypeStruct((n, d), F32),
        scratch_shapes=[pltpu.VMEM((2, TOP_K, rows, d), F32), pltpu.SemaphoreType.DMA((2,))],
        compiler_params=_cparams("arbitrary"),
        name="moe_combine",
    )(dest, dest, h, gate, ys)


def _to_time_major(x_ref, slab):
    nb, tt, d = x_ref.shape
    for b in range(nb):
        for j in range(d // LANES):
            slab[j, pl.ds(b, tt, stride=nb), :] = x_ref[b, :, j * LANES:(j + 1) * LANES]
    return jnp.concatenate([slab[j] for j in range(d // LANES)], axis=1)


def _residual_from_time_major(y, x_ref, o_ref, slab):
    nb, tt, d = x_ref.shape
    for j in range(d // LANES):
        slab[j] = y[:, j * LANES:(j + 1) * LANES]
    for b in range(nb):
        for j in range(d // LANES):
            cs = slice(j * LANES, (j + 1) * LANES)
            o_ref[b, :, cs] = x_ref[b, :, cs] + slab[j, pl.ds(b, tt, stride=nb), :]


def _norm_kernel(h_ref, g_ref, o_ref):
    o_ref[...] = _rms(h_ref[...], g_ref[...]).astype(o_ref.dtype)


def _rmsnorm(h, g, dtype, *, tm=1024):
    n, d = h.shape
    row = lambda i: (i, 0)
    return pl.pallas_call(
        _norm_kernel,
        grid=(n // tm,),
        in_specs=[pl.BlockSpec((tm, d), row), _resident((1, d))],
        out_specs=pl.BlockSpec((tm, d), row),
        out_shape=jax.ShapeDtypeStruct((n, d), dtype),
        compiler_params=_cparams("parallel"),
        name="rmsnorm",
    )(h, g.reshape(1, d))


def _s5_prep_kernel(ar_ref, ai_ref, ld_ref, brt_ref, bit_ref, bbr_ref, bbi_ref, lr_ref, li_ref):
    ar = ar_ref[...]
    ai = ai_ref[...]
    dt = jnp.exp(ld_ref[...])
    mag = jnp.exp(ar * dt)
    lr = mag * jnp.cos(ai * dt)
    li = mag * jnp.sin(ai * dt)
    lr_ref[...] = lr
    li_ref[...] = li
    den = ar * ar + ai * ai
    nr = lr - 1.0
    kr = ((nr * ar + li * ai) / den)[:, None, :]
    ki = ((li * ar - nr * ai) / den)[:, None, :]
    brt = brt_ref[...]
    bit = bit_ref[...]
    bbr_ref[...] = kr * brt - ki * bit
    bbi_ref[...] = kr * bit + ki * brt


def _s5_kernel(h_ref, g_ref, bre_ref, bim_ref, cre_ref, cim_ref, lr_ref, li_ref, d_ref, w_ref, o_ref,
               slab_in, slab_out, s_re, s_im, c_re, c_im):
    nb, tt, d = h_ref.shape
    nslab, _, sw = bre_ref.shape

    @pl.when(pl.program_id(0) == 0)
    def _():
        c_re[...] = jnp.zeros_like(c_re)
        c_im[...] = jnp.zeros_like(c_im)

    hn = _rms(_to_time_major(h_ref, slab_in), g_ref[...])
    hb = hn.astype(BF16)
    for j in range(nslab):
        xj = hb[:, j * LANES:(j + 1) * LANES]
        s_re[j] = _dot(xj, bre_ref[j])
        s_im[j] = _dot(xj, bim_ref[j])

    ys = []
    for j in range(nslab):
        lr = jnp.broadcast_to(lr_ref[j:j + 1, :], (nb, sw))
        li = jnp.broadcast_to(li_ref[j:j + 1, :], (nb, sw))

        def step(t, carry, j=j, lr=lr, li=li):
            cr, ci = carry
            rows = pl.ds(pl.multiple_of(t * nb, nb), nb)
            nr = lr * cr - li * ci + s_re[j, rows, :]
            ni = lr * ci + li * cr + s_im[j, rows, :]
            s_re[j, rows, :] = nr
            s_im[j, rows, :] = ni
            return nr, ni

        cr, ci = lax.fori_loop(0, tt, step, (c_re[j], c_im[j]), unroll=True)
        c_re[j] = cr
        c_im[j] = ci
        ys.append(_dot(s_re[j].astype(BF16), cre_ref[j]) - _dot(s_im[j].astype(BF16), cim_ref[j]))

    y = jnp.concatenate(ys, axis=1) + d_ref[...] * hn
    ab = _dot(_gelu(y).astype(BF16), w_ref[...])
    _residual_from_time_major(ab[:, :d] * jax.nn.sigmoid(ab[:, d:]), h_ref, o_ref, slab_out)


def _block_diag(blocks, per):
    n, a, b = blocks.shape
    eye = jnp.eye(per, dtype=blocks.dtype)
    out = blocks.reshape(n // per, per, a, 1, b) * eye[None, :, None, :, None]
    return out.reshape(n // per, per * a, per * b)


def _mixer_s5(h, g, bsz, a_re, a_im, log_dt, b_re, b_im, c_re, c_im, d_skip, w_glu, *, tt=64):
    n, d = h.shape
    seq = n // bsz
    ng, p = a_re.shape
    per = LANES // S5_GROUP
    nslab = ng // per
    sw = per * p
    full = lambda *shape: pl.BlockSpec(shape, lambda i: (0,) * len(shape))
    bbr, bbi, lr, li = pl.pallas_call(
        _s5_prep_kernel,
        grid=(1,),
        in_specs=[full(ng, p)] * 3 + [full(ng, S5_GROUP, p)] * 2,
        out_specs=[full(ng, S5_GROUP, p)] * 2 + [full(ng, p)] * 2,
        out_shape=[jax.ShapeDtypeStruct((ng, S5_GROUP, p), F32)] * 2 + [jax.ShapeDtypeStruct((ng, p), F32)] * 2,
        name="s5_prep",
    )(a_re, a_im, jnp.broadcast_to(log_dt[:, None], (ng, p)), jnp.swapaxes(b_re, 1, 2), jnp.swapaxes(b_im, 1, 2))
    bre = _block_diag(bbr, per).astype(BF16)
    bim = _block_diag(bbi, per).astype(BF16)
    cre = _block_diag(jnp.swapaxes(c_re, 1, 2), per).astype(BF16)
    cim = _block_diag(jnp.swapaxes(c_im, 1, 2), per).astype(BF16)
    blk = pl.BlockSpec((bsz, tt, d), lambda i: (0, i, 0))
    out = pl.pallas_call(
        _s5_kernel,
        grid=(seq // tt,),
        in_specs=[blk, _resident((1, d)), _resident((nslab, LANES, sw)), _resident((nslab, LANES, sw)),
                  _resident((nslab, sw, LANES)), _resident((nslab, sw, LANES)), _resident((nslab, sw)),
                  _resident((nslab, sw)), _resident((1, d)), _resident((d, 2 * d))],
        out_specs=blk,
        out_shape=jax.ShapeDtypeStruct((bsz, seq, d), F32),
        scratch_shapes=[pltpu.VMEM((d // LANES, tt * bsz, LANES), F32)] * 2
        + [pltpu.VMEM((nslab, tt * bsz, sw), F32)] * 2 + [pltpu.VMEM((nslab, bsz, sw), F32)] * 2,
        compiler_params=_cparams("arbitrary"),
        name="s5_mixer",
    )(h.reshape(bsz, seq, d), g.reshape(1, d), bre, bim, cre, cim, lr.reshape(nslab, sw), li.reshape(nslab, sw),
      d_skip.reshape(1, d), w_glu.astype(BF16))
    return out.reshape(n, d)


def _rglru_kernel(h_ref, g_ref, win_ref, cw_ref, cb_ref, wr_ref, br_ref, wi_ref, bi_ref, lam_ref, wout_ref, o_ref,
                  slab_in, slab_out, hist, state, a_scr, b_scr):
    nb, tt, d = h_ref.shape
    r = wout_ref.shape[0]
    kw = cw_ref.shape[0]
    nh = wr_ref.shape[0]
    hd = r // nh
    rows = tt * nb

    @pl.when(pl.program_id(0) == 0)
    def _():
        hist[...] = jnp.zeros_like(hist)
        state[...] = jnp.zeros_like(state)

    hn = _rms(_to_time_major(h_ref, slab_in), g_ref[...]).astype(BF16)
    xb = _dot(hn, win_ref[:, :r])
    gate = _gelu(_dot(hn, win_ref[:, r:]))
    xpad = jnp.concatenate([hist[...], xb], axis=0)
    hist[...] = xb[rows - (kw - 1) * nb:]
    xc = cb_ref[...]
    for k in range(kw):
        xc = xc + cw_ref[k:k + 1, :] * xpad[k * nb:k * nb + rows]
    lam = -lam_ref[...]
    sp = jnp.maximum(lam, 0.0) + jnp.log1p(jnp.exp(-jnp.abs(lam)))
    for hh in range(nh):
        cs = slice(hh * hd, (hh + 1) * hd)
        xh = xc[:, cs]
        xhb = xh.astype(BF16)
        rg = jax.nn.sigmoid(_dot(xhb, wr_ref[hh]) + br_ref[:, cs])
        ig = jax.nn.sigmoid(_dot(xhb, wi_ref[hh]) + bi_ref[:, cs])
        log_a = (-LRU_C) * rg * sp[:, cs]
        a_scr[:, cs] = jnp.exp(log_a)
        b_scr[:, cs] = jnp.sqrt(1.0 - jnp.exp(2.0 * log_a)) * (ig * xh)

    def step(t, hcur):
        sl = pl.ds(pl.multiple_of(t * nb, nb), nb)
        hnew = a_scr[sl, :] * hcur + b_scr[sl, :]
        b_scr[sl, :] = hnew
        return hnew

    state[...] = lax.fori_loop(0, tt, step, state[...], unroll=True)
    y = (b_scr[...] * gate).astype(BF16)
    _residual_from_time_major(_dot(y, wout_ref[...]), h_ref, o_ref, slab_out)


def _mixer_rglru(h, g, bsz, w_in, conv_w, conv_b, w_r, b_r, w_i, b_i, lam, w_out, *, tt=64):
    n, d = h.shape
    seq = n // bsz
    r = w_out.shape[0]
    nh, hd, _ = w_r.shape
    kw = conv_w.shape[0]
    blk = pl.BlockSpec((bsz, tt, d), lambda i: (0, i, 0))
    out = pl.pallas_call(
        _rglru_kernel,
        grid=(seq // tt,),
        in_specs=[blk, _resident((1, d)), _resident((d, 2 * r)), _resident((kw, r)), _resident((1, r)),
                  _resident((nh, hd, hd)), _resident((1, r)), _resident((nh, hd, hd)), _resident((1, r)),
                  _resident((1, r)), _resident((r, d))],
        out_specs=blk,
        out_shape=jax.ShapeDtypeStruct((bsz, seq, d), F32),
        scratch_shapes=[pltpu.VMEM((d // LANES, tt * bsz, LANES), F32)] * 2
        + [pltpu.VMEM(((kw - 1) * bsz, r), F32), pltpu.VMEM((bsz, r), F32),
           pltpu.VMEM((tt * bsz, r), F32), pltpu.VMEM((tt * bsz, r), F32)],
        compiler_params=_cparams("arbitrary"),
        name="rglru_mixer",
    )(h.reshape(bsz, seq, d), g.reshape(1, d), w_in.astype(BF16), conv_w, conv_b.reshape(1, r), w_r.astype(BF16),
      b_r.reshape(1, r), w_i.astype(BF16), b_i.reshape(1, r), lam.reshape(1, r), w_out.astype(BF16))
    return out.reshape(n, d)


def kernel(x, norm_mix_g, norm_ffn_g, norm_final_g, a_w_in, a_ln_g, a_ln_b, a_w_s, a_b_s, a_w_out, b_a_re, b_a_im, b_log_dt, b_b_re, b_b_im, b_c_re, b_c_im, b_d, b_w_glu, c_w_in, c_conv_w, c_conv_b, c_w_r, c_b_r, c_w_i, c_b_i, c_lam, c_w_out, f_w_gu, f_w_down, m_w_router, m_w_gu, m_w_down):
    bsz, seq, d = x.shape
    depth = norm_mix_g.shape[0]
    h = x.reshape(bsz * seq, d)
    for i in range(depth):
        kind = i % N_MIXERS
        j = i // N_MIXERS
        if kind == 0:
            h = _mixer_gmlp(h, norm_mix_g[i], a_w_in[j], a_ln_g[j], a_ln_b[j], a_w_s[j], a_b_s[j], a_w_out[j])
        elif kind == 1:
            h = _mixer_s5(h, norm_mix_g[i], bsz, b_a_re[j], b_a_im[j], b_log_dt[j], b_b_re[j], b_b_im[j],
                          b_c_re[j], b_c_im[j], b_d[j], b_w_glu[j])
        else:
            h = _mixer_rglru(h, norm_mix_g[i], bsz, c_w_in[j], c_conv_w[j], c_conv_b[j], c_w_r[j], c_b_r[j],
                             c_w_i[j], c_b_i[j], c_lam[j], c_w_out[j])
        k = i // 2
        if i % 2 == 0:
            h = _ffn_dense(h, norm_ffn_g[i], f_w_gu[k], f_w_down[k])
        else:
            h = _moe(h, norm_ffn_g[i], k, m_w_router[k], m_w_gu, m_w_down)
    return _rmsnorm(h, norm_final_g, x.dtype).reshape(bsz, seq, d)
```

```python
import functools
import math

import jax
import jax.numpy as jnp
from jax import lax
from jax.experimental import pallas as pl
from jax.experimental.pallas import tpu as pltpu

F32 = jnp.float32
BF16 = jnp.bfloat16
EPS = 1e-6
N_MIXERS = 3
CHUNK = 128
N_GROUPS_A = 8
S5_GROUP = 16
N_HEADS_C = 12
LRU_C = 8.0
TOP_K = 2
LANES = 128
MXU_TILE = 256
VMEM_LIMIT = 56 * 2**20


def _cparams(*sem):
    return pltpu.CompilerParams(dimension_semantics=sem, vmem_limit_bytes=VMEM_LIMIT)


def _resident(shape):
    nd = len(shape)
    return pl.BlockSpec(shape, lambda *_: (0,) * nd, pipeline_mode=pl.Buffered(1))


def _rms(x, g):
    return x * lax.rsqrt(jnp.mean(x * x, axis=-1, keepdims=True) + EPS) * g


def _gelu(x):
    c = math.sqrt(2.0 / math.pi)
    return x * (0.5 * (1.0 + jnp.tanh(c * (x + 0.044715 * (x * x * x)))))


def _dot(a, b):
    return jnp.dot(a, b, preferred_element_type=F32)


def _mxu_chunks(size, n):
    tiles = size // MXU_TILE
    assert tiles * MXU_TILE == size
    return [MXU_TILE * (tiles // n + (1 if i < tiles % n else 0)) for i in range(n)]


def _gmlp_in_kernel(h_ref, g_ref, w_ref, lng_ref, lnb_ref, u_ref, v_ref):
    e = u_ref.shape[-1]
    hn = _rms(h_ref[...], g_ref[...]).astype(BF16)
    u_ref[...] = _gelu(_dot(hn, w_ref[:, :e])).astype(BF16)
    v = _gelu(_dot(hn, w_ref[:, e:]))
    vc = v - jnp.mean(v, axis=-1, keepdims=True)
    var = jnp.mean(vc * vc, axis=-1, keepdims=True)
    v_ref[...] = (vc * lax.rsqrt(var + EPS) * lng_ref[...] + lnb_ref[...]).astype(BF16)


def _gmlp_out_kernel(u_ref, v_ref, ws_ref, bs_ref, wo_ref, h_ref, o_ref, y_scr):
    tc, e = u_ref.shape
    ng = ws_ref.shape[0]
    ge = e // ng
    for c in range(tc // CHUNK):
        r = slice(c * CHUNK, (c + 1) * CHUNK)
        for g in range(ng):
            cs = slice(g * ge, (g + 1) * ge)
            sv = _dot(ws_ref[g], v_ref[r, cs]) + bs_ref[:, g:g + 1]
            y_scr[r, cs] = (u_ref[r, cs].astype(F32) * sv).astype(BF16)
    o_ref[...] = h_ref[...] + _dot(y_scr[...], wo_ref[...])


def _mixer_gmlp(h, g, w_in, ln_g, ln_b, w_s, b_s, w_out, *, tm=512):
    n, d = h.shape
    e = w_out.shape[0]
    ng = w_s.shape[0]
    row = lambda i: (i, 0)
    u, v = pl.pallas_call(
        _gmlp_in_kernel,
        grid=(n // tm,),
        in_specs=[pl.BlockSpec((tm, d), row), _resident((1, d)), _resident((d, 2 * e)),
                  _resident((1, e)), _resident((1, e))],
        out_specs=[pl.BlockSpec((tm, e), row), pl.BlockSpec((tm, e), row)],
        out_shape=[jax.ShapeDtypeStruct((n, e), BF16)] * 2,
        compiler_params=_cparams("parallel"),
        name="gmlp_in",
    )(h, g.reshape(1, d), w_in.astype(BF16), ln_g.reshape(1, e), ln_b.reshape(1, e))
    mask = jnp.tril(jnp.ones((CHUNK, CHUNK), dtype=bool))
    ws = jnp.where(mask[None], w_s, 0.0).astype(BF16)
    return pl.pallas_call(
        _gmlp_out_kernel,
        grid=(n // tm,),
        in_specs=[pl.BlockSpec((tm, e), row), pl.BlockSpec((tm, e), row), _resident((ng, CHUNK, CHUNK)),
                  _resident((CHUNK, ng)), _resident((e, d)), pl.BlockSpec((tm, d), row)],
        out_specs=pl.BlockSpec((tm, d), row),
        out_shape=jax.ShapeDtypeStruct((n, d), F32),
        scratch_shapes=[pltpu.VMEM((tm, e), BF16)],
        compiler_params=_cparams("parallel"),
        name="gmlp_out",
    )(u, v, ws, b_s.T, w_out.astype(BF16), h)


def _ffn_kernel(h_ref, g_ref, wgu_ref, wd_ref, o_ref, *, chunks):
    x = h_ref[...]
    hn = _rms(x, g_ref[...]).astype(BF16)
    f = wd_ref.shape[0]
    acc = x
    f0 = 0
    for fc in chunks:
        gg = _dot(hn, wgu_ref[:, f0:f0 + fc])
        uu = _dot(hn, wgu_ref[:, f + f0:f + f0 + fc])
        a = (gg * jax.nn.sigmoid(gg) * uu).astype(BF16)
        acc = acc + _dot(a, wd_ref[f0:f0 + fc, :])
        f0 += fc
    o_ref[...] = acc


def _ffn_dense(h, g, w_gu, w_down, *, tm=512):
    n, d = h.shape
    f = w_down.shape[0]
    row = lambda i: (i, 0)
    return pl.pallas_call(
        functools.partial(_ffn_kernel, chunks=_mxu_chunks(f, 2)),
        grid=(n // tm,),
        in_specs=[pl.BlockSpec((tm, d), row), _resident((1, d)), _resident((d, 2 * f)), _resident((f, d))],
        out_specs=pl.BlockSpec((tm, d), row),
        out_shape=jax.ShapeDtypeStruct((n, d), F32),
        compiler_params=_cparams("parallel"),
        name="ffn_dense",
    )(h, g.reshape(1, d), w_gu.astype(BF16), w_down.astype(BF16))


def _route_kernel(h_ref, g_ref, wrt_ref, idx_ref, gate_ref):
    hn = _rms(h_ref[...], g_ref[...])
    logits = lax.dot_general(wrt_ref[...], hn, (((1,), (1,)), ((), ())), preferred_element_type=F32,
                             precision=lax.Precision.HIGHEST)
    ne = logits.shape[0]
    eid = lax.broadcasted_iota(jnp.int32, logits.shape, 0)
    neg = jnp.float32(-jnp.inf)
    m1 = jnp.max(logits, axis=0, keepdims=True)
    i1 = jnp.min(jnp.where(logits == m1, eid, ne), axis=0, keepdims=True)
    lg2 = jnp.where(eid == i1, neg, logits)
    m2 = jnp.max(lg2, axis=0, keepdims=True)
    i2 = jnp.min(jnp.where(lg2 == m2, eid, ne), axis=0, keepdims=True)
    ex = jnp.exp(m2 - m1)
    idx_ref[...] = jnp.concatenate([i1, i2], axis=0)
    gate_ref[...] = jnp.concatenate([1.0 / (1.0 + ex), ex / (1.0 + ex)], axis=0)


def _dispatch_kernel(dst_ref, pad_ref, x_ref, o_hbm, zbuf, sem, zsem, *, tm):
    i = pl.program_id(0)
    rows = x_ref.shape[0]
    zrows = zbuf.shape[0]

    @pl.when(i == 0)
    def _():
        zbuf[...] = jnp.zeros_like(zbuf)
        for start_wait in (True, False):
            for e in range(pad_ref.shape[0]):
                first = pad_ref[e]

                @pl.when(first >= 0)
                def _():
                    base = pl.multiple_of(jnp.maximum(first, 0), zrows)
                    for q in range(tm // zrows):
                        cp = pltpu.make_async_copy(zbuf, o_hbm.at[pl.ds(base + q * zrows, zrows), :], zsem)
                        cp.start() if start_wait else cp.wait()

    def row_copy(r, slot_row):
        return pltpu.make_async_copy(x_ref.at[pl.ds(r, 1), :], o_hbm.at[pl.ds(slot_row, 1), :], sem)

    for r in range(rows):
        for k in range(TOP_K):
            row_copy(r, dst_ref[0, 0, TOP_K * r + k]).start()
    for r in range(rows):
        for k in range(TOP_K):
            row_copy(r, 0).wait()


def _expert_kernel(te_ref, nu_ref, x_ref, g_ref, wg_ref, wu_ref, wd_ref, o_ref, xn_scr):
    i = pl.program_id(0)
    f = pl.program_id(1)
    used = i < nu_ref[0]

    @pl.when(f == 0)
    def _():
        o_ref[...] = jnp.zeros_like(o_ref)

    @pl.when(jnp.logical_and(used, f == 0))
    def _():
        xn_scr[...] = _rms(x_ref[...], g_ref[...]).astype(BF16)

    @pl.when(used)
    def _():
        xn = xn_scr[...]
        gg = _dot(xn, wg_ref[0, 0].astype(BF16))
        uu = _dot(xn, wu_ref[0, 0].astype(BF16))
        a = (gg * jax.nn.sigmoid(gg) * uu).astype(BF16)
        o_ref[...] += _dot(a, wd_ref[0, 0].astype(BF16))


def _combine_kernel(cur_ref, nxt_ref, h_ref, gate_ref, y_hbm, o_ref, buf, sems):
    i = pl.program_id(0)
    nstep = pl.num_programs(0)
    rows = h_ref.shape[0]

    def row_copy(src_row, par, k, r):
        return pltpu.make_async_copy(y_hbm.at[pl.ds(src_row, 1), :], buf.at[par, k, pl.ds(r, 1), :], sems.at[par])

    def issue(idx_ref, par):
        for r in range(rows):
            for k in range(TOP_K):
                row_copy(idx_ref[0, 0, TOP_K * r + k], par, k, r).start()

    def drain(par):
        for r in range(rows):
            for k in range(TOP_K):
                row_copy(0, par, k, r).wait()

    @pl.when(i == 0)
    def _():
        issue(cur_ref, 0)

    for par in range(2):
        @pl.when(i % 2 == par)
        def _():
            @pl.when(i + 1 < nstep)
            def _():
                issue(nxt_ref, 1 - par)

            drain(par)
            gate = gate_ref[...]
            o_ref[...] = h_ref[...] + gate[:, 0:1] * buf[par, 0] + gate[:, 1:2] * buf[par, 1]


def _moe(h, g, layer, w_router, w_gu, w_down, *, tm=1024, tf=512, rows=256, drows=512, zrows=256):
    n, d = h.shape
    ne = w_router.shape[1]
    f = w_down.shape[2]
    nf = f // tf
    row = lambda i: (i, 0)
    idx, gate = pl.pallas_call(
        _route_kernel,
        grid=(n // 512,),
        in_specs=[pl.BlockSpec((512, d), row), _resident((1, d)), _resident((ne, d))],
        out_specs=[pl.BlockSpec((TOP_K, 512), lambda i: (0, i)), pl.BlockSpec((TOP_K, 512), lambda i: (0, i))],
        out_shape=[jax.ShapeDtypeStruct((TOP_K, n), jnp.int32), jax.ShapeDtypeStruct((TOP_K, n), F32)],
        compiler_params=_cparams("parallel"),
        name="moe_route",
    )(h, g.reshape(1, d), w_router.T)
    gate = jnp.zeros((n, LANES), F32).at[:, :TOP_K].set(gate.T)

    npair = n * TOP_K
    ntile = npair // tm + ne
    e_flat = idx.T.reshape(npair)
    onehot = (e_flat[:, None] == jnp.arange(ne, dtype=jnp.int32)[None, :]).astype(jnp.int32)
    csum = jnp.cumsum(onehot, axis=0)
    rank = jnp.sum((csum - onehot) * onehot, axis=1)
    counts = csum[-1]
    padded = ((counts + tm - 1) // tm) * tm
    ends = jnp.cumsum(padded)
    dest = (ends - padded)[e_flat] + rank
    tile_start = jnp.arange(ntile, dtype=jnp.int32) * tm
    spare = ends[-1] + tile_start[:ne]
    pad_tile = jnp.concatenate([jnp.where(padded > 0, ends - tm, -1),
                                jnp.where(spare < ntile * tm, spare, -1)]).astype(jnp.int32)
    tile_expert = jnp.minimum(jnp.sum((tile_start[:, None] >= ends[None, :]).astype(jnp.int32), axis=1), ne - 1)
    n_used = (ends[-1] // tm).astype(jnp.int32).reshape(1)

    nblk = n // rows
    pair_blk = lambda r: pl.BlockSpec((1, 1, TOP_K * r), lambda i: (i, 0, 0), memory_space=pltpu.SMEM)
    xs = pl.pallas_call(
        functools.partial(_dispatch_kernel, tm=tm),
        grid=(n // drows,),
        in_specs=[pair_blk(drows), pl.BlockSpec(memory_space=pltpu.SMEM), pl.BlockSpec((drows, d), row)],
        out_specs=pl.BlockSpec(memory_space=pl.ANY),
        out_shape=jax.ShapeDtypeStruct((ntile * tm, d), F32),
        scratch_shapes=[pltpu.VMEM((zrows, d), F32), pltpu.SemaphoreType.DMA(()), pltpu.SemaphoreType.DMA(())],
        compiler_params=_cparams("arbitrary"),
        name="moe_dispatch",
    )(dest.reshape(n // drows, 1, TOP_K * drows), pad_tile, h)
    dest = dest.reshape(nblk, 1, TOP_K * rows)

    def wsel(off):
        def index(i, j, te, nu):
            return (layer, te[i], 0, off + jnp.where(i < nu[0], j, nf - 1))
        return index

    def wdsel(i, j, te, nu):
        return (layer, te[i], jnp.where(i < nu[0], j, nf - 1), 0)

    ys = pl.pallas_call(
        _expert_kernel,
        grid_spec=pltpu.PrefetchScalarGridSpec(
            num_scalar_prefetch=2,
            grid=(ntile, nf),
            in_specs=[pl.BlockSpec((tm, d), lambda i, j, te, nu: (jnp.where(i < nu[0], i, 0), 0)),
                      pl.BlockSpec((1, d), lambda i, j, te, nu: (0, 0)),
                      pl.BlockSpec((1, 1, d, tf), wsel(0)),
                      pl.BlockSpec((1, 1, d, tf), wsel(nf)),
                      pl.BlockSpec((1, 1, tf, d), wdsel)],
            out_specs=pl.BlockSpec((tm, d), lambda i, j, te, nu: (i, 0)),
            scratch_shapes=[pltpu.VMEM((tm, d), BF16)],
        ),
        out_shape=jax.ShapeDtypeStruct((ntile * tm, d), F32),
        compiler_params=_cparams("arbitrary", "arbitrary"),
        name="moe_expert",
    )(tile_expert, n_used, xs, g.reshape(1, d), w_gu, w_gu, w_down)

    return pl.pallas_call(
        _combine_kernel,
        grid=(nblk,),
        in_specs=[pair_blk(rows),
                  pl.BlockSpec((1, 1, TOP_K * rows), lambda i: (jnp.minimum(i + 1, nblk - 1), 0, 0),
                               memory_space=pltpu.SMEM),
                  pl.BlockSpec((rows, d), row), pl.BlockSpec((rows, LANES), row),
                  pl.BlockSpec(memory_space=pl.ANY)],
        out_specs=pl.BlockSpec((rows, d), row),
        out_shape=jax.ShapeDtypeStruct((n, d), F32),
        scratch_shapes=[pltpu.VMEM((2, TOP_K, rows, d), F32), pltpu.SemaphoreType.DMA((2,))],
        compiler_params=_cparams("arbitrary"),
        name="moe_combine",
    )(dest, dest, h, gate, ys)


def _to_time_major(x_ref, slab):
    nb, tt, d = x_ref.shape
    for b in range(nb):
        for j in range(d // LANES):
            slab[j, pl.ds(b, tt, stride=nb), :] = x_ref[b, :, j * LANES:(j + 1) * LANES]
    return jnp.concatenate([slab[j] for j in range(d // LANES)], axis=1)


def _residual_from_time_major(y, x_ref, o_ref, slab):
    nb, tt, d = x_ref.shape
    for j in range(d // LANES):
        slab[j] = y[:, j * LANES:(j + 1) * LANES]
    for b in range(nb):
        for j in range(d // LANES):
            cs = slice(j * LANES, (j + 1) * LANES)
            o_ref[b, :, cs] = x_ref[b, :, cs] + slab[j, pl.ds(b, tt, stride=nb), :]


def _norm_kernel(h_ref, g_ref, o_ref):
    o_ref[...] = _rms(h_ref[...], g_ref[...]).astype(o_ref.dtype)


def _rmsnorm(h, g, dtype, *, tm=1024):
    n, d = h.shape
    row = lambda i: (i, 0)
    return pl.pallas_call(
        _norm_kernel,
        grid=(n // tm,),
        in_specs=[pl.BlockSpec((tm, d), row), _resident((1, d))],
        out_specs=pl.BlockSpec((tm, d), row),
        out_shape=jax.ShapeDtypeStruct((n, d), dtype),
        compiler_params=_cparams("parallel"),
        name="rmsnorm",
    )(h, g.reshape(1, d))


def _s5_prep_kernel(ar_ref, ai_ref, ld_ref, brt_ref, bit_ref, bbr_ref, bbi_ref, lr_ref, li_ref):
    ar = ar_ref[...]
    ai = ai_ref[...]
    dt = jnp.exp(ld_ref[...])
    mag = jnp.exp(ar * dt)
    lr = mag * jnp.cos(ai * dt)
    li = mag * jnp.sin(ai * dt)
    lr_ref[...] = lr
    li_ref[...] = li
    den = ar * ar + ai * ai
    nr = lr - 1.0
    kr = ((nr * ar + li * ai) / den)[:, None, :]
    ki = ((li * ar - nr * ai) / den)[:, None, :]
    brt = brt_ref[...]
    bit = bit_ref[...]
    bbr_ref[...] = kr * brt - ki * bit
    bbi_ref[...] = kr * bit + ki * brt


def _s5_kernel(h_ref, g_ref, bre_ref, bim_ref, cre_ref, cim_ref, lr_ref, li_ref, d_ref, w_ref, o_ref,
               slab_in, slab_out, s_re, s_im, c_re, c_im):
    nb, tt, d = h_ref.shape
    nslab, _, sw = bre_ref.shape

    @pl.when(pl.program_id(0) == 0)
    def _():
        c_re[...] = jnp.zeros_like(c_re)
        c_im[...] = jnp.zeros_like(c_im)

    hn = _rms(_to_time_major(h_ref, slab_in), g_ref[...])
    hb = hn.astype(BF16)
    for j in range(nslab):
        xj = hb[:, j * LANES:(j + 1) * LANES]
        s_re[j] = _dot(xj, bre_ref[j])
        s_im[j] = _dot(xj, bim_ref[j])

    ys = []
    for j in range(nslab):
        lr = jnp.broadcast_to(lr_ref[j:j + 1, :], (nb, sw))
        li = jnp.broadcast_to(li_ref[j:j + 1, :], (nb, sw))

        def step(t, carry, j=j, lr=lr, li=li):
            cr, ci = carry
            rows = pl.ds(pl.multiple_of(t * nb, nb), nb)
            nr = lr * cr - li * ci + s_re[j, rows, :]
            ni = lr * ci + li * cr + s_im[j, rows, :]
            s_re[j, rows, :] = nr
            s_im[j, rows, :] = ni
            return nr, ni

        cr, ci = lax.fori_loop(0, tt, step, (c_re[j], c_im[j]), unroll=True)
        c_re[j] = cr
        c_im[j] = ci
        ys.append(_dot(s_re[j].astype(BF16), cre_ref[j]) - _dot(s_im[j].astype(BF16), cim_ref[j]))

    y = jnp.concatenate(ys, axis=1) + d_ref[...] * hn
    ab = _dot(_gelu(y).astype(BF16), w_ref[...])
    _residual_from_time_major(ab[:, :d] * jax.nn.sigmoid(ab[:, d:]), h_ref, o_ref, slab_out)


def _block_diag(blocks, per):
    n, a, b = blocks.shape
    eye = jnp.eye(per, dtype=blocks.dtype)
    out = blocks.reshape(n // per, per, a, 1, b) * eye[None, :, None, :, None]
    return out.reshape(n // per, per * a, per * b)


def _mixer_s5(h, g, bsz, a_re, a_im, log_dt, b_re, b_im, c_re, c_im, d_skip, w_glu, *, tt=64):
    n, d = h.shape
    seq = n // bsz
    ng, p = a_re.shape
    per = LANES // S5_GROUP
    nslab = ng // per
    sw = per * p
    full = lambda *shape: pl.BlockSpec(shape, lambda i: (0,) * len(shape))
    bbr, bbi, lr, li = pl.pallas_call(
        _s5_prep_kernel,
        grid=(1,),
        in_specs=[full(ng, p)] * 3 + [full(ng, S5_GROUP, p)] * 2,
        out_specs=[full(ng, S5_GROUP, p)] * 2 + [full(ng, p)] * 2,
        out_shape=[jax.ShapeDtypeStruct((ng, S5_GROUP, p), F32)] * 2 + [jax.ShapeDtypeStruct((ng, p), F32)] * 2,
        name="s5_prep",
    )(a_re, a_im, jnp.broadcast_to(log_dt[:, None], (ng, p)), jnp.swapaxes(b_re, 1, 2), jnp.swapaxes(b_im, 1, 2))
    bre = _block_diag(bbr, per).astype(BF16)
    bim = _block_diag(bbi, per).astype(BF16)
    cre = _block_diag(jnp.swapaxes(c_re, 1, 2), per).astype(BF16)
    cim = _block_diag(jnp.swapaxes(c_im, 1, 2), per).astype(BF16)
    blk = pl.BlockSpec((bsz, tt, d), lambda i: (0, i, 0))
    out = pl.pallas_call(
        _s5_kernel,
        grid=(seq // tt,),
        in_specs=[blk, _resident((1, d)), _resident((nslab, LANES, sw)), _resident((nslab, LANES, sw)),
                  _resident((nslab, sw, LANES)), _resident((nslab, sw, LANES)), _resident((nslab, sw)),
                  _resident((nslab, sw)), _resident((1, d)), _resident((d, 2 * d))],
        out_specs=blk,
        out_shape=jax.ShapeDtypeStruct((bsz, seq, d), F32),
        scratch_shapes=[pltpu.VMEM((d // LANES, tt * bsz, LANES), F32)] * 2
        + [pltpu.VMEM((nslab, tt * bsz, sw), F32)] * 2 + [pltpu.VMEM((nslab, bsz, sw), F32)] * 2,
        compiler_params=_cparams("arbitrary"),
        name="s5_mixer",
    )(h.reshape(bsz, seq, d), g.reshape(1, d), bre, bim, cre, cim, lr.reshape(nslab, sw), li.reshape(nslab, sw),
      d_skip.reshape(1, d), w_glu.astype(BF16))
    return out.reshape(n, d)


def _rglru_kernel(h_ref, g_ref, win_ref, cw_ref, cb_ref, wr_ref, br_ref, wi_ref, bi_ref, lam_ref, wout_ref, o_ref,
                  slab_in, slab_out, hist, state, a_scr, b_scr):
    nb, tt, d = h_ref.shape
    r = wout_ref.shape[0]
    kw = cw_ref.shape[0]
    nh = wr_ref.shape[0]
    hd = r // nh
    rows = tt * nb

    @pl.when(pl.program_id(0) == 0)
    def _():
        hist[...] = jnp.zeros_like(hist)
        state[...] = jnp.zeros_like(state)

    hn = _rms(_to_time_major(h_ref, slab_in), g_ref[...]).astype(BF16)
    xb = _dot(hn, win_ref[:, :r])
    gate = _gelu(_dot(hn, win_ref[:, r:]))
    xpad = jnp.concatenate([hist[...], xb], axis=0)
    hist[...] = xb[rows - (kw - 1) * nb:]
    xc = cb_ref[...]
    for k in range(kw):
        xc = xc + cw_ref[k:k + 1, :] * xpad[k * nb:k * nb + rows]
    lam = -lam_ref[...]
    sp = jnp.maximum(lam, 0.0) + jnp.log1p(jnp.exp(-jnp.abs(lam)))
    for hh in range(nh):
        cs = slice(hh * hd, (hh + 1) * hd)
        xh = xc[:, cs]
        xhb = xh.astype(BF16)
        rg = jax.nn.sigmoid(_dot(xhb, wr_ref[hh]) + br_ref[:, cs])
        ig = jax.nn.sigmoid(_dot(xhb, wi_ref[hh]) + bi_ref[:, cs])
        log_a = (-LRU_C) * rg * sp[:, cs]
        a_scr[:, cs] = jnp.exp(log_a)
        b_scr[:, cs] = jnp.sqrt(1.0 - jnp.exp(2.0 * log_a)) * (ig * xh)

    def step(t, hcur):
        sl = pl.ds(pl.multiple_of(t * nb, nb), nb)
        hnew = a_scr[sl, :] * hcur + b_scr[sl, :]
        b_scr[sl, :] = hnew
        return hnew

    state[...] = lax.fori_loop(0, tt, step, state[...], unroll=True)
    y = (b_scr[...] * gate).astype(BF16)
    _residual_from_time_major(_dot(y, wout_ref[...]), h_ref, o_ref, slab_out)


def _mixer_rglru(h, g, bsz, w_in, conv_w, conv_b, w_r, b_r, w_i, b_i, lam, w_out, *, tt=64):
    n, d = h.shape
    seq = n // bsz
    r = w_out.shape[0]
    nh, hd, _ = w_r.shape
    kw = conv_w.shape[0]
    blk = pl.BlockSpec((bsz, tt, d), lambda i: (0, i, 0))
    out = pl.pallas_call(
        _rglru_kernel,
        grid=(seq // tt,),
        in_specs=[blk, _resident((1, d)), _resident((d, 2 * r)), _resident((kw, r)), _resident((1, r)),
                  _resident((nh, hd, hd)), _resident((1, r)), _resident((nh, hd, hd)), _resident((1, r)),
                  _resident((1, r)), _resident((r, d))],
        out_specs=blk,
        out_shape=jax.ShapeDtypeStruct((bsz, seq, d), F32),
        scratch_shapes=[pltpu.VMEM((d // LANES, tt * bsz, LANES), F32)] * 2
        + [pltpu.VMEM(((kw - 1) * bsz, r), F32), pltpu.VMEM((bsz, r), F32),
           pltpu.VMEM((tt * bsz, r), F32), pltpu.VMEM((tt * bsz, r), F32)],
        compiler_params=_cparams("arbitrary"),
        name="rglru_mixer",
    )(h.reshape(bsz, seq, d), g.reshape(1, d), w_in.astype(BF16), conv_w, conv_b.reshape(1, r), w_r.astype(BF16),
      b_r.reshape(1, r), w_i.astype(BF16), b_i.reshape(1, r), lam.reshape(1, r), w_out.astype(BF16))
    return out.reshape(n, d)


def kernel(x, norm_mix_g, norm_ffn_g, norm_final_g, a_w_in, a_ln_g, a_ln_b, a_w_s, a_b_s, a_w_out, b_a_re, b_a_im, b_log_dt, b_b_re, b_b_im, b_c_re, b_c_im, b_d, b_w_glu, c_w_in, c_conv_w, c_conv_b, c_w_r, c_b_r, c_w_i, c_b_i, c_lam, c_w_out, f_w_gu, f_w_down, m_w_router, m_w_gu, m_w_down):
    bsz, seq, d = x.shape
    depth = norm_mix_g.shape[0]
    h = x.reshape(bsz * seq, d)
    for i in range(depth):
        kind = i % N_MIXERS
        j = i // N_MIXERS
        if kind == 0:
            h = _mixer_gmlp(h, norm_mix_g[i], a_w_in[j], a_ln_g[j], a_ln_b[j], a_w_s[j], a_b_s[j], a_w_out[j])
        elif kind == 1:
            h = _mixer_s5(h, norm_mix_g[i], bsz, b_a_re[j], b_a_im[j], b_log_dt[j], b_b_re[j], b_b_im[j],
                          b_c_re[j], b_c_im[j], b_d[j], b_w_glu[j])
        else:
            h = _mixer_rglru(h, norm_mix_g[i], bsz, c_w_in[j], c_conv_w[j], c_conv_b[j], c_w_r[j], c_b_r[j],
                             c_w_i[j], c_b_i[j], c_lam[j], c_w_out[j])
        k = i // 2
        if i % 2 == 0:
            h = _ffn_dense(h, norm_ffn_g[i], f_w_gu[k], f_w_down[k])
        else:
            h = _moe(h, norm_ffn_g[i], k, m_w_router[k], m_w_gu, m_w_down)
    return _rmsnorm(h, norm_final_g, x.dtype).reshape(bsz, seq, d)
```

```python
import functools
import math

import jax
import jax.numpy as jnp
from jax import lax
from jax.experimental import pallas as pl
from jax.experimental.pallas import tpu as pltpu

F32 = jnp.float32
BF16 = jnp.bfloat16
EPS = 1e-6
N_MIXERS = 3
CHUNK = 128
N_GROUPS_A = 8
S5_GROUP = 16
N_HEADS_C = 12
LRU_C = 8.0
TOP_K = 2
LANES = 128
MXU_TILE = 256
VMEM_LIMIT = 56 * 2**20


def _cparams(*sem):
    return pltpu.CompilerParams(dimension_semantics=sem, vmem_limit_bytes=VMEM_LIMIT)


def _resident(shape):
    nd = len(shape)
    return pl.BlockSpec(shape, lambda *_: (0,) * nd, pipeline_mode=pl.Buffered(1))


def _rms(x, g):
    return x * lax.rsqrt(jnp.mean(x * x, axis=-1, keepdims=True) + EPS) * g


def _gelu(x):
    c = math.sqrt(2.0 / math.pi)
    return (0.5 * x) * (1.0 + jnp.tanh(x * (c + (c * 0.044715) * (x * x))))


def _dot(a, b):
    return jnp.dot(a, b, preferred_element_type=F32)


def _mxu_chunks(size, n):
    tiles = size // MXU_TILE
    assert tiles * MXU_TILE == size
    return [MXU_TILE * (tiles // n + (1 if i < tiles % n else 0)) for i in range(n)]


def _gmlp_in_kernel(h_ref, g_ref, w_ref, lng_ref, lnb_ref, u_ref, v_ref):
    e = u_ref.shape[-1]
    hn = _rms(h_ref[...], g_ref[...]).astype(BF16)
    v = _gelu(_dot(hn, w_ref[:, e:]))
    vc = v - jnp.mean(v, axis=-1, keepdims=True)
    var = jnp.mean(vc * vc, axis=-1, keepdims=True)
    v_ref[...] = (vc * lax.rsqrt(var + EPS) * lng_ref[...] + lnb_ref[...]).astype(BF16)
    u_ref[...] = _gelu(_dot(hn, w_ref[:, :e])).astype(BF16)


def _gmlp_out_kernel(u_ref, v_ref, ws_ref, bs_ref, wo_ref, h_ref, o_ref, y_scr):
    tc, e = u_ref.shape
    ng = ws_ref.shape[0]
    ge = e // ng
    for c in range(tc // CHUNK):
        r = slice(c * CHUNK, (c + 1) * CHUNK)
        for g in range(ng):
            cs = slice(g * ge, (g + 1) * ge)
            sv = _dot(ws_ref[g], v_ref[r, cs]) + bs_ref[:, g:g + 1]
            y_scr[r, cs] = (u_ref[r, cs].astype(F32) * sv).astype(BF16)
    o_ref[...] = h_ref[...] + _dot(y_scr[...], wo_ref[...])


def _mixer_gmlp(h, g, w_in, ln_g, ln_b, w_s, b_s, w_out, *, tm=512):
    n, d = h.shape
    e = w_out.shape[0]
    ng = w_s.shape[0]
    row = lambda i: (i, 0)
    u, v = pl.pallas_call(
        _gmlp_in_kernel,
        grid=(n // tm,),
        in_specs=[pl.BlockSpec((tm, d), row), _resident((1, d)), _resident((d, 2 * e)),
                  _resident((1, e)), _resident((1, e))],
        out_specs=[pl.BlockSpec((tm, e), row), pl.BlockSpec((tm, e), row)],
        out_shape=[jax.ShapeDtypeStruct((n, e), BF16)] * 2,
        compiler_params=_cparams("parallel"),
        name="gmlp_in",
    )(h, g.reshape(1, d), w_in.astype(BF16), ln_g.reshape(1, e), ln_b.reshape(1, e))
    mask = jnp.tril(jnp.ones((CHUNK, CHUNK), dtype=bool))
    ws = jnp.where(mask[None], w_s, 0.0).astype(BF16)
    return pl.pallas_call(
        _gmlp_out_kernel,
        grid=(n // tm,),
        in_specs=[pl.BlockSpec((tm, e), row), pl.BlockSpec((tm, e), row), _resident((ng, CHUNK, CHUNK)),
                  _resident((CHUNK, ng)), _resident((e, d)), pl.BlockSpec((tm, d), row)],
        out_specs=pl.BlockSpec((tm, d), row),
        out_shape=jax.ShapeDtypeStruct((n, d), F32),
        scratch_shapes=[pltpu.VMEM((tm, e), BF16)],
        compiler_params=_cparams("parallel"),
        name="gmlp_out",
    )(u, v, ws, b_s.T, w_out.astype(BF16), h)


def _ffn_kernel(h_ref, g_ref, wgu_ref, wd_ref, og_ref, o_ref, *, chunks, final):
    x = h_ref[...]
    hn = _rms(x, g_ref[...]).astype(BF16)
    f = wd_ref.shape[0]
    acc = x
    f0 = 0
    for fc in chunks:
        gg = _dot(hn, wgu_ref[:, f0:f0 + fc])
        uu = _dot(hn, wgu_ref[:, f + f0:f + f0 + fc])
        a = (gg * jax.nn.sigmoid(gg) * uu).astype(BF16)
        acc = acc + _dot(a, wd_ref[f0:f0 + fc, :])
        f0 += fc
    o_ref[...] = _rms(acc, og_ref[...]) if final else acc


def _ffn_dense(h, g, w_gu, w_down, out_g, final, *, tm=512):
    n, d = h.shape
    f = w_down.shape[0]
    row = lambda i: (i, 0)
    return pl.pallas_call(
        functools.partial(_ffn_kernel, chunks=_mxu_chunks(f, 2), final=final),
        grid=(n // tm,),
        in_specs=[pl.BlockSpec((tm, d), row), _resident((1, d)), _resident((d, 2 * f)), _resident((f, d)),
                  _resident((1, d))],
        out_specs=pl.BlockSpec((tm, d), row),
        out_shape=jax.ShapeDtypeStruct((n, d), F32),
        compiler_params=_cparams("parallel"),
        name="ffn_dense",
    )(h, g.reshape(1, d), w_gu.astype(BF16), w_down.astype(BF16), out_g.reshape(1, d))


def _route_kernel(h_ref, g_ref, wrt_ref, idx_ref, gate_ref):
    hn = _rms(h_ref[...], g_ref[...])
    logits = lax.dot_general(wrt_ref[...], hn, (((1,), (1,)), ((), ())), preferred_element_type=F32,
                             precision=lax.Precision.HIGHEST)
    ne = logits.shape[0]
    eid = lax.broadcasted_iota(jnp.int32, logits.shape, 0)
    neg = jnp.float32(-jnp.inf)
    m1 = jnp.max(logits, axis=0, keepdims=True)
    i1 = jnp.min(jnp.where(logits == m1, eid, ne), axis=0, keepdims=True)
    lg2 = jnp.where(eid == i1, neg, logits)
    m2 = jnp.max(lg2, axis=0, keepdims=True)
    i2 = jnp.min(jnp.where(lg2 == m2, eid, ne), axis=0, keepdims=True)
    ex = jnp.exp(m2 - m1)
    idx_ref[...] = jnp.concatenate([i1, i2], axis=0)
    gate_ref[...] = jnp.concatenate([1.0 / (1.0 + ex), ex / (1.0 + ex)], axis=0)


def _dispatch_kernel(dst_ref, pad_ref, x_ref, o_hbm, zbuf, sem, zsem, *, tm):
    i = pl.program_id(0)
    rows = x_ref.shape[0]
    zrows = zbuf.shape[0]

    @pl.when(i == 0)
    def _():
        zbuf[...] = jnp.zeros_like(zbuf)
        for start_wait in (True, False):
            for e in range(pad_ref.shape[0]):
                first = pad_ref[e]

                @pl.when(first >= 0)
                def _():
                    base = pl.multiple_of(jnp.maximum(first, 0), zrows)
                    for q in range(tm // zrows):
                        cp = pltpu.make_async_copy(zbuf, o_hbm.at[pl.ds(base + q * zrows, zrows), :], zsem)
                        cp.start() if start_wait else cp.wait()

    def row_copy(r, slot_row):
        return pltpu.make_async_copy(x_ref.at[pl.ds(r, 1), :], o_hbm.at[pl.ds(slot_row, 1), :], sem)

    for r in range(rows):
        for k in range(TOP_K):
            row_copy(r, dst_ref[0, 0, TOP_K * r + k]).start()
    for r in range(rows):
        for k in range(TOP_K):
            row_copy(r, 0).wait()


def _expert_kernel(te_ref, nu_ref, x_ref, g_ref, wg_ref, wu_ref, wd_ref, o_ref, xn_scr):
    i = pl.program_id(0)
    f = pl.program_id(1)
    used = i < nu_ref[0]

    @pl.when(f == 0)
    def _():
        o_ref[...] = jnp.zeros_like(o_ref)

    @pl.when(jnp.logical_and(used, f == 0))
    def _():
        xn_scr[...] = _rms(x_ref[...], g_ref[...]).astype(BF16)

    @pl.when(used)
    def _():
        xn = xn_scr[...]
        gg = _dot(xn, wg_ref[0, 0].astype(BF16))
        uu = _dot(xn, wu_ref[0, 0].astype(BF16))
        a = (gg * jax.nn.sigmoid(gg) * uu).astype(BF16)
        o_ref[...] += _dot(a, wd_ref[0, 0].astype(BF16))


def _combine_kernel(cur_ref, nxt_ref, h_ref, gate_ref, y_hbm, og_ref, o_ref, buf, sems, *, final):
    i = pl.program_id(0)
    nstep = pl.num_programs(0)
    rows = h_ref.shape[0]

    def row_copy(src_row, par, k, r):
        return pltpu.make_async_copy(y_hbm.at[pl.ds(src_row, 1), :], buf.at[par, k, pl.ds(r, 1), :], sems.at[par])

    def issue(idx_ref, par):
        for r in range(rows):
            for k in range(TOP_K):
                row_copy(idx_ref[0, 0, TOP_K * r + k], par, k, r).start()

    def drain(par):
        for r in range(rows):
            for k in range(TOP_K):
                row_copy(0, par, k, r).wait()

    @pl.when(i == 0)
    def _():
        issue(cur_ref, 0)

    for par in range(2):
        @pl.when(i % 2 == par)
        def _():
            @pl.when(i + 1 < nstep)
            def _():
                issue(nxt_ref, 1 - par)

            drain(par)
            gate = gate_ref[...]
            out = h_ref[...] + gate[:, 0:1] * buf[par, 0] + gate[:, 1:2] * buf[par, 1]
            o_ref[...] = _rms(out, og_ref[...]) if final else out


def _moe(h, g, layer, w_router, w_gu, w_down, out_g, final, *, tm=1024, tf=512, rows=256, drows=1024, zrows=256):
    n, d = h.shape
    ne = w_router.shape[1]
    f = w_down.shape[2]
    nf = f // tf
    row = lambda i: (i, 0)
    idx, gate = pl.pallas_call(
        _route_kernel,
        grid=(n // 512,),
        in_specs=[pl.BlockSpec((512, d), row), _resident((1, d)), _resident((ne, d))],
        out_specs=[pl.BlockSpec((TOP_K, 512), lambda i: (0, i)), pl.BlockSpec((TOP_K, 512), lambda i: (0, i))],
        out_shape=[jax.ShapeDtypeStruct((TOP_K, n), jnp.int32), jax.ShapeDtypeStruct((TOP_K, n), F32)],
        compiler_params=_cparams("parallel"),
        name="moe_route",
    )(h, g.reshape(1, d), w_router.T)
    gate = jnp.zeros((n, LANES), F32).at[:, :TOP_K].set(gate.T)

    npair = n * TOP_K
    ntile = npair // tm + ne
    e_flat = idx.T.reshape(npair)
    onehot = (e_flat[:, None] == jnp.arange(ne, dtype=jnp.int32)[None, :]).astype(jnp.int32)
    csum = jnp.cumsum(onehot, axis=0)
    rank = jnp.sum((csum - onehot) * onehot, axis=1)
    counts = csum[-1]
    padded = ((counts + tm - 1) // tm) * tm
    ends = jnp.cumsum(padded)
    dest = (ends - padded)[e_flat] + rank
    tile_start = jnp.arange(ntile, dtype=jnp.int32) * tm
    spare = ends[-1] + tile_start[:ne]
    pad_tile = jnp.concatenate([jnp.where(padded > 0, ends - tm, -1),
                                jnp.where(spare < ntile * tm, spare, -1)]).astype(jnp.int32)
    tile_expert = jnp.minimum(jnp.sum((tile_start[:, None] >= ends[None, :]).astype(jnp.int32), axis=1), ne - 1)
    n_used = (ends[-1] // tm).astype(jnp.int32).reshape(1)

    nblk = n // rows
    pair_blk = lambda r: pl.BlockSpec((1, 1, TOP_K * r), lambda i: (i, 0, 0), memory_space=pltpu.SMEM)
    xs = pl.pallas_call(
        functools.partial(_dispatch_kernel, tm=tm),
        grid=(n // drows,),
        in_specs=[pair_blk(drows), pl.BlockSpec(memory_space=pltpu.SMEM), pl.BlockSpec((drows, d), row)],
        out_specs=pl.BlockSpec(memory_space=pl.ANY),
        out_shape=jax.ShapeDtypeStruct((ntile * tm, d), F32),
        scratch_shapes=[pltpu.VMEM((zrows, d), F32), pltpu.SemaphoreType.DMA(()), pltpu.SemaphoreType.DMA(())],
        compiler_params=_cparams("arbitrary"),
        name="moe_dispatch",
    )(dest.reshape(n // drows, 1, TOP_K * drows), pad_tile, h)
    dest = dest.reshape(nblk, 1, TOP_K * rows)

    def wsel(off):
        def index(i, j, te, nu):
            return (layer, te[i], 0, off + jnp.where(i < nu[0], j, nf - 1))
        return index

    def wdsel(i, j, te, nu):
        return (layer, te[i], jnp.where(i < nu[0], j, nf - 1), 0)

    ys = pl.pallas_call(
        _expert_kernel,
        grid_spec=pltpu.PrefetchScalarGridSpec(
            num_scalar_prefetch=2,
            grid=(ntile, nf),
            in_specs=[pl.BlockSpec((tm, d), lambda i, j, te, nu: (jnp.where(i < nu[0], i, 0), 0)),
                      pl.BlockSpec((1, d), lambda i, j, te, nu: (0, 0)),
                      pl.BlockSpec((1, 1, d, tf), wsel(0)),
                      pl.BlockSpec((1, 1, d, tf), wsel(nf)),
                      pl.BlockSpec((1, 1, tf, d), wdsel)],
            out_specs=pl.BlockSpec((tm, d), lambda i, j, te, nu: (i, 0)),
            scratch_shapes=[pltpu.VMEM((tm, d), BF16)],
        ),
        out_shape=jax.ShapeDtypeStruct((ntile * tm, d), F32),
        compiler_params=_cparams("arbitrary", "arbitrary"),
        name="moe_expert",
    )(tile_expert, n_used, xs, g.reshape(1, d), w_gu, w_gu, w_down)

    return pl.pallas_call(
        functools.partial(_combine_kernel, final=final),
        grid=(nblk,),
        in_specs=[pair_blk(rows),
                  pl.BlockSpec((1, 1, TOP_K * rows), lambda i: (jnp.minimum(i + 1, nblk - 1), 0, 0),
                               memory_space=pltpu.SMEM),
                  pl.BlockSpec((rows, d), row), pl.BlockSpec((rows, LANES), row),
                  pl.BlockSpec(memory_space=pl.ANY), _resident((1, d))],
        out_specs=pl.BlockSpec((rows, d), row),
        out_shape=jax.ShapeDtypeStruct((n, d), F32),
        scratch_shapes=[pltpu.VMEM((2, TOP_K, rows, d), F32), pltpu.SemaphoreType.DMA((2,))],
        compiler_params=_cparams("arbitrary"),
        name="moe_combine",
    )(dest, dest, h, gate, ys, out_g.reshape(1, d))


def _to_time_major(x_ref, slab):
    nb, tt, d = x_ref.shape
    for b in range(nb):
        for j in range(d // LANES):
            slab[j, pl.ds(b, tt, stride=nb), :] = x_ref[b, :, j * LANES:(j + 1) * LANES]
    return jnp.concatenate([slab[j] for j in range(d // LANES)], axis=1)


def _residual_from_time_major(y, x_ref, o_ref, slab):
    nb, tt, d = x_ref.shape
    for j in range(d // LANES):
        slab[j] = y[:, j * LANES:(j + 1) * LANES]
    for b in range(nb):
        for j in range(d // LANES):
            cs = slice(j * LANES, (j + 1) * LANES)
            o_ref[b, :, cs] = x_ref[b, :, cs] + slab[j, pl.ds(b, tt, stride=nb), :]


def _s5_prep_kernel(ar_ref, ai_ref, ld_ref, brt_ref, bit_ref, bbr_ref, bbi_ref, lr_ref, li_ref):
    ar = ar_ref[...]
    ai = ai_ref[...]
    dt = jnp.exp(ld_ref[...])
    mag = jnp.exp(ar * dt)
    lr = mag * jnp.cos(ai * dt)
    li = mag * jnp.sin(ai * dt)
    lr_ref[...] = lr
    li_ref[...] = li
    den = ar * ar + ai * ai
    nr = lr - 1.0
    kr = ((nr * ar + li * ai) / den)[:, None, :]
    ki = ((li * ar - nr * ai) / den)[:, None, :]
    brt = brt_ref[...]
    bit = bit_ref[...]
    bbr_ref[...] = kr * brt - ki * bit
    bbi_ref[...] = kr * bit + ki * brt


def _s5_kernel(h_ref, g_ref, bre_ref, bim_ref, cre_ref, cim_ref, lr_ref, li_ref, d_ref, w_ref, o_ref,
               slab_in, slab_out, s_re, s_im, c_re, c_im):
    nb, tt, d = h_ref.shape
    nslab, _, sw = bre_ref.shape

    @pl.when(pl.program_id(0) == 0)
    def _():
        c_re[...] = jnp.zeros_like(c_re)
        c_im[...] = jnp.zeros_like(c_im)

    hn = _rms(_to_time_major(h_ref, slab_in), g_ref[...])
    hb = hn.astype(BF16)
    for j in range(nslab):
        xj = hb[:, j * LANES:(j + 1) * LANES]
        s_re[j] = _dot(xj, bre_ref[j])
        s_im[j] = _dot(xj, bim_ref[j])

    ys = []
    for j in range(nslab):
        lr = jnp.broadcast_to(lr_ref[j:j + 1, :], (nb, sw))
        li = jnp.broadcast_to(li_ref[j:j + 1, :], (nb, sw))

        def step(t, carry, j=j, lr=lr, li=li):
            cr, ci = carry
            rows = pl.ds(pl.multiple_of(t * nb, nb), nb)
            nr = lr * cr - li * ci + s_re[j, rows, :]
            ni = lr * ci + li * cr + s_im[j, rows, :]
            s_re[j, rows, :] = nr
            s_im[j, rows, :] = ni
            return nr, ni

        cr, ci = lax.fori_loop(0, tt, step, (c_re[j], c_im[j]), unroll=True)
        c_re[j] = cr
        c_im[j] = ci
        ys.append(_dot(s_re[j].astype(BF16), cre_ref[j]) - _dot(s_im[j].astype(BF16), cim_ref[j]))

    y = jnp.concatenate(ys, axis=1) + d_ref[...] * hn
    ab = _dot(_gelu(y).astype(BF16), w_ref[...])
    _residual_from_time_major(ab[:, :d] * jax.nn.sigmoid(ab[:, d:]), h_ref, o_ref, slab_out)


def _block_diag(blocks, per):
    n, a, b = blocks.shape
    eye = jnp.eye(per, dtype=blocks.dtype)
    out = blocks.reshape(n // per, per, a, 1, b) * eye[None, :, None, :, None]
    return out.reshape(n // per, per * a, per * b)


def _mixer_s5(h, g, bsz, a_re, a_im, log_dt, b_re, b_im, c_re, c_im, d_skip, w_glu, *, tt=64):
    n, d = h.shape
    seq = n // bsz
    ng, p = a_re.shape
    per = LANES // S5_GROUP
    nslab = ng // per
    sw = per * p
    full = lambda *shape: pl.BlockSpec(shape, lambda i: (0,) * len(shape))
    bbr, bbi, lr, li = pl.pallas_call(
        _s5_prep_kernel,
        grid=(1,),
        in_specs=[full(ng, p)] * 3 + [full(ng, S5_GROUP, p)] * 2,
        out_specs=[full(ng, S5_GROUP, p)] * 2 + [full(ng, p)] * 2,
        out_shape=[jax.ShapeDtypeStruct((ng, S5_GROUP, p), F32)] * 2 + [jax.ShapeDtypeStruct((ng, p), F32)] * 2,
        name="s5_prep",
    )(a_re, a_im, jnp.broadcast_to(log_dt[:, None], (ng, p)), jnp.swapaxes(b_re, 1, 2), jnp.swapaxes(b_im, 1, 2))
    bre = _block_diag(bbr, per).astype(BF16)
    bim = _block_diag(bbi, per).astype(BF16)
    cre = _block_diag(jnp.swapaxes(c_re, 1, 2), per).astype(BF16)
    cim = _block_diag(jnp.swapaxes(c_im, 1, 2), per).astype(BF16)
    blk = pl.BlockSpec((bsz, tt, d), lambda i: (0, i, 0))
    out = pl.pallas_call(
        _s5_kernel,
        grid=(seq // tt,),
        in_specs=[blk, _resident((1, d)), _resident((nslab, LANES, sw)), _resident((nslab, LANES, sw)),
                  _resident((nslab, sw, LANES)), _resident((nslab, sw, LANES)), _resident((nslab, sw)),
                  _resident((nslab, sw)), _resident((1, d)), _resident((d, 2 * d))],
        out_specs=blk,
        out_shape=jax.ShapeDtypeStruct((bsz, seq, d), F32),
        scratch_shapes=[pltpu.VMEM((d // LANES, tt * bsz, LANES), F32)] * 2
        + [pltpu.VMEM((nslab, tt * bsz, sw), F32)] * 2 + [pltpu.VMEM((nslab, bsz, sw), F32)] * 2,
        compiler_params=_cparams("arbitrary"),
        name="s5_mixer",
    )(h.reshape(bsz, seq, d), g.reshape(1, d), bre, bim, cre, cim, lr.reshape(nslab, sw), li.reshape(nslab, sw),
      d_skip.reshape(1, d), w_glu.astype(BF16))
    return out.reshape(n, d)


def _rglru_kernel(h_ref, g_ref, win_ref, cw_ref, cb_ref, wr_ref, br_ref, wi_ref, bi_ref, lam_ref, wout_ref, o_ref,
                  slab_in, slab_out, hist, state, a_scr, b_scr):
    nb, tt, d = h_ref.shape
    r = wout_ref.shape[0]
    kw = cw_ref.shape[0]
    nh = wr_ref.shape[0]
    hd = r // nh
    rows = tt * nb

    @pl.when(pl.program_id(0) == 0)
    def _():
        hist[...] = jnp.zeros_like(hist)
        state[...] = jnp.zeros_like(state)

    hn = _rms(_to_time_major(h_ref, slab_in), g_ref[...]).astype(BF16)
    xb = _dot(hn, win_ref[:, :r])
    gate = _gelu(_dot(hn, win_ref[:, r:]))
    xpad = jnp.concatenate([hist[...], xb], axis=0)
    hist[...] = xb[rows - (kw - 1) * nb:]
    xc = cb_ref[...]
    for k in range(kw):
        xc = xc + cw_ref[k:k + 1, :] * xpad[k * nb:k * nb + rows]
    lam = -lam_ref[...]
    sp = jnp.maximum(lam, 0.0) + jnp.log1p(jnp.exp(-jnp.abs(lam)))
    for hh in range(nh):
        cs = slice(hh * hd, (hh + 1) * hd)
        xh = xc[:, cs]
        xhb = xh.astype(BF16)
        rg = jax.nn.sigmoid(_dot(xhb, wr_ref[hh]) + br_ref[:, cs])
        ig = jax.nn.sigmoid(_dot(xhb, wi_ref[hh]) + bi_ref[:, cs])
        a = jnp.exp((-LRU_C) * rg * sp[:, cs])
        a_scr[:, cs] = a
        b_scr[:, cs] = jnp.sqrt(1.0 - a * a) * (ig * xh)

    def step(t, hcur):
        sl = pl.ds(pl.multiple_of(t * nb, nb), nb)
        hnew = a_scr[sl, :] * hcur + b_scr[sl, :]
        b_scr[sl, :] = hnew
        return hnew

    state[...] = lax.fori_loop(0, tt, step, state[...], unroll=True)
    y = (b_scr[...] * gate).astype(BF16)
    _residual_from_time_major(_dot(y, wout_ref[...]), h_ref, o_ref, slab_out)


def _mixer_rglru(h, g, bsz, w_in, conv_w, conv_b, w_r, b_r, w_i, b_i, lam, w_out, *, tt=64):
    n, d = h.shape
    seq = n // bsz
    r = w_out.shape[0]
    nh, hd, _ = w_r.shape
    kw = conv_w.shape[0]
    blk = pl.BlockSpec((bsz, tt, d), lambda i: (0, i, 0))
    out = pl.pallas_call(
        _rglru_kernel,
        grid=(seq // tt,),
        in_specs=[blk, _resident((1, d)), _resident((d, 2 * r)), _resident((kw, r)), _resident((1, r)),
                  _resident((nh, hd, hd)), _resident((1, r)), _resident((nh, hd, hd)), _resident((1, r)),
                  _resident((1, r)), _resident((r, d))],
        out_specs=blk,
        out_shape=jax.ShapeDtypeStruct((bsz, seq, d), F32),
        scratch_shapes=[pltpu.VMEM((d // LANES, tt * bsz, LANES), F32)] * 2
        + [pltpu.VMEM(((kw - 1) * bsz, r), F32), pltpu.VMEM((bsz, r), F32),
           pltpu.VMEM((tt * bsz, r), F32), pltpu.VMEM((tt * bsz, r), F32)],
        compiler_params=_cparams("arbitrary"),
        name="rglru_mixer",
    )(h.reshape(bsz, seq, d), g.reshape(1, d), w_in.astype(BF16), conv_w, conv_b.reshape(1, r), w_r.astype(BF16),
      b_r.reshape(1, r), w_i.astype(BF16), b_i.reshape(1, r), lam.reshape(1, r), w_out.astype(BF16))
    return out.reshape(n, d)


def kernel(x, norm_mix_g, norm_ffn_g, norm_final_g, a_w_in, a_ln_g, a_ln_b, a_w_s, a_b_s, a_w_out, b_a_re, b_a_im, b_log_dt, b_b_re, b_b_im, b_c_re, b_c_im, b_d, b_w_glu, c_w_in, c_conv_w, c_conv_b, c_w_r, c_b_r, c_w_i, c_b_i, c_lam, c_w_out, f_w_gu, f_w_down, m_w_router, m_w_gu, m_w_down):
    bsz, seq, d = x.shape
    depth = norm_mix_g.shape[0]
    h = x.reshape(bsz * seq, d)
    for i in range(depth):
        kind = i % N_MIXERS
        j = i // N_MIXERS
        if kind == 0:
            h = _mixer_gmlp(h, norm_mix_g[i], a_w_in[j], a_ln_g[j], a_ln_b[j], a_w_s[j], a_b_s[j], a_w_out[j])
        elif kind == 1:
            h = _mixer_s5(h, norm_mix_g[i], bsz, b_a_re[j], b_a_im[j], b_log_dt[j], b_b_re[j], b_b_im[j],
                          b_c_re[j], b_c_im[j], b_d[j], b_w_glu[j])
        else:
            h = _mixer_rglru(h, norm_mix_g[i], bsz, c_w_in[j], c_conv_w[j], c_conv_b[j], c_w_r[j], c_b_r[j],
                             c_w_i[j], c_b_i[j], c_lam[j], c_w_out[j])
        k = i // 2
        final = i == depth - 1
        if i % 2 == 0:
            h = _ffn_dense(h, norm_ffn_g[i], f_w_gu[k], f_w_down[k], norm_final_g, final)
        else:
            h = _moe(h, norm_ffn_g[i], k, m_w_router[k], m_w_gu, m_w_down, norm_final_g, final)
    return h.reshape(bsz, seq, d)
```

```python
import functools
import math

import jax
import jax.numpy as jnp
from jax import lax
from jax.experimental import pallas as pl
from jax.experimental.pallas import tpu as pltpu

F32 = jnp.float32
BF16 = jnp.bfloat16
EPS = 1e-6
N_MIXERS = 3
CHUNK = 128
N_GROUPS_A = 8
S5_GROUP = 16
N_HEADS_C = 12
LRU_C = 8.0
TOP_K = 2
LANES = 128
MXU_TILE = 256
VMEM_LIMIT = 56 * 2**20


def _cparams(*sem):
    return pltpu.CompilerParams(dimension_semantics=sem, vmem_limit_bytes=VMEM_LIMIT)


def _resident(shape):
    nd = len(shape)
    return pl.BlockSpec(shape, lambda *_: (0,) * nd, pipeline_mode=pl.Buffered(1))


def _rms(x, g):
    return x * lax.rsqrt(jnp.mean(x * x, axis=-1, keepdims=True) + EPS) * g


def _gelu(x):
    c = math.sqrt(2.0 / math.pi)
    return (0.5 * x) * (1.0 + jnp.tanh(x * (c + (c * 0.044715) * (x * x))))


def _dot(a, b):
    return jnp.dot(a, b, preferred_element_type=F32)


def _mxu_chunks(size, n):
    tiles = size // MXU_TILE
    assert tiles * MXU_TILE == size
    return [MXU_TILE * (tiles // n + (1 if i < tiles % n else 0)) for i in range(n)]


def _gmlp_in_kernel(h_ref, g_ref, w_ref, lng_ref, lnb_ref, u_ref, v_ref):
    e = u_ref.shape[-1]
    hn = _rms(h_ref[...], g_ref[...]).astype(BF16)
    v = _gelu(_dot(hn, w_ref[:, e:]))
    vc = v - jnp.mean(v, axis=-1, keepdims=True)
    var = jnp.mean(vc * vc, axis=-1, keepdims=True)
    v_ref[...] = (vc * lax.rsqrt(var + EPS) * lng_ref[...] + lnb_ref[...]).astype(BF16)
    u_ref[...] = _gelu(_dot(hn, w_ref[:, :e])).astype(BF16)


def _gmlp_out_kernel(u_ref, v_ref, ws_ref, bs_ref, wo_ref, h_ref, o_ref, y_scr):
    tc, e = u_ref.shape
    ng = ws_ref.shape[0]
    ge = e // ng
    for c in range(tc // CHUNK):
        r = slice(c * CHUNK, (c + 1) * CHUNK)
        for g in range(ng):
            cs = slice(g * ge, (g + 1) * ge)
            sv = _dot(ws_ref[g], v_ref[r, cs]) + bs_ref[:, g:g + 1]
            y_scr[r, cs] = (u_ref[r, cs].astype(F32) * sv).astype(BF16)
    o_ref[...] = h_ref[...] + _dot(y_scr[...], wo_ref[...])


def _mixer_gmlp(h, g, w_in, ln_g, ln_b, w_s, b_s, w_out, *, tm=512):
    n, d = h.shape
    e = w_out.shape[0]
    ng = w_s.shape[0]
    row = lambda i: (i, 0)
    u, v = pl.pallas_call(
        _gmlp_in_kernel,
        grid=(n // tm,),
        in_specs=[pl.BlockSpec((tm, d), row), _resident((1, d)), _resident((d, 2 * e)),
                  _resident((1, e)), _resident((1, e))],
        out_specs=[pl.BlockSpec((tm, e), row), pl.BlockSpec((tm, e), row)],
        out_shape=[jax.ShapeDtypeStruct((n, e), BF16)] * 2,
        compiler_params=_cparams("parallel"),
        name="gmlp_in",
    )(h, g.reshape(1, d), w_in.astype(BF16), ln_g.reshape(1, e), ln_b.reshape(1, e))
    mask = jnp.tril(jnp.ones((CHUNK, CHUNK), dtype=bool))
    ws = jnp.where(mask[None], w_s, 0.0).astype(BF16)
    return pl.pallas_call(
        _gmlp_out_kernel,
        grid=(n // tm,),
        in_specs=[pl.BlockSpec((tm, e), row), pl.BlockSpec((tm, e), row), _resident((ng, CHUNK, CHUNK)),
                  _resident((CHUNK, ng)), _resident((e, d)), pl.BlockSpec((tm, d), row)],
        out_specs=pl.BlockSpec((tm, d), row),
        out_shape=jax.ShapeDtypeStruct((n, d), F32),
        scratch_shapes=[pltpu.VMEM((tm, e), BF16)],
        compiler_params=_cparams("parallel"),
        name="gmlp_out",
    )(u, v, ws, b_s.T, w_out.astype(BF16), h)


def _ffn_kernel(h_ref, g_ref, wgu_ref, wd_ref, og_ref, o_ref, *, chunks, final):
    x = h_ref[...]
    hn = _rms(x, g_ref[...]).astype(BF16)
    f = wd_ref.shape[0]
    acc = x
    f0 = 0
    for fc in chunks:
        gg = _dot(hn, wgu_ref[:, f0:f0 + fc])
        uu = _dot(hn, wgu_ref[:, f + f0:f + f0 + fc])
        a = (gg * jax.nn.sigmoid(gg) * uu).astype(BF16)
        acc = acc + _dot(a, wd_ref[f0:f0 + fc, :])
        f0 += fc
    o_ref[...] = _rms(acc, og_ref[...]) if final else acc


def _ffn_dense(h, g, w_gu, w_down, out_g, final, *, tm=512):
    n, d = h.shape
    f = w_down.shape[0]
    row = lambda i: (i, 0)
    return pl.pallas_call(
        functools.partial(_ffn_kernel, chunks=_mxu_chunks(f, 2), final=final),
        grid=(n // tm,),
        in_specs=[pl.BlockSpec((tm, d), row), _resident((1, d)), _resident((d, 2 * f)), _resident((f, d)),
                  _resident((1, d))],
        out_specs=pl.BlockSpec((tm, d), row),
        out_shape=jax.ShapeDtypeStruct((n, d), F32),
        compiler_params=_cparams("parallel"),
        name="ffn_dense",
    )(h, g.reshape(1, d), w_gu.astype(BF16), w_down.astype(BF16), out_g.reshape(1, d))


def _route_kernel(h_ref, g_ref, wrt_ref, idx_ref, gate_ref):
    hn = _rms(h_ref[...], g_ref[...])
    logits = lax.dot_general(wrt_ref[...], hn, (((1,), (1,)), ((), ())), preferred_element_type=F32,
                             precision=lax.Precision.HIGHEST)
    ne = logits.shape[0]
    eid = lax.broadcasted_iota(jnp.int32, logits.shape, 0)
    neg = jnp.float32(-jnp.inf)
    m1 = jnp.max(logits, axis=0, keepdims=True)
    i1 = jnp.min(jnp.where(logits == m1, eid, ne), axis=0, keepdims=True)
    lg2 = jnp.where(eid == i1, neg, logits)
    m2 = jnp.max(lg2, axis=0, keepdims=True)
    i2 = jnp.min(jnp.where(lg2 == m2, eid, ne), axis=0, keepdims=True)
    ex = jnp.exp(m2 - m1)
    idx_ref[...] = jnp.concatenate([i1, i2], axis=0)
    gates = jnp.concatenate([1.0 / (1.0 + ex), ex / (1.0 + ex), jnp.zeros((LANES - TOP_K, ex.shape[1]), F32)], axis=0)
    gate_ref[...] = gates.T


def _dispatch_kernel(dst_ref, pad_ref, x_ref, o_hbm, zbuf, sem, zsem, *, tm):
    i = pl.program_id(0)
    rows = x_ref.shape[0]
    zrows = zbuf.shape[0]

    @pl.when(i == 0)
    def _():
        zbuf[...] = jnp.zeros_like(zbuf)
        for start_wait in (True, False):
            for e in range(pad_ref.shape[0]):
                first = pad_ref[e]

                @pl.when(first >= 0)
                def _():
                    base = pl.multiple_of(jnp.maximum(first, 0), zrows)
                    for q in range(tm // zrows):
                        cp = pltpu.make_async_copy(zbuf, o_hbm.at[pl.ds(base + q * zrows, zrows), :], zsem)
                        cp.start() if start_wait else cp.wait()

    def row_copy(r, slot_row):
        return pltpu.make_async_copy(x_ref.at[pl.ds(r, 1), :], o_hbm.at[pl.ds(slot_row, 1), :], sem)

    for r in range(rows):
        for k in range(TOP_K):
            row_copy(r, dst_ref[0, 0, TOP_K * r + k]).start()
    for r in range(rows):
        for k in range(TOP_K):
            row_copy(r, 0).wait()


def _expert_kernel(te_ref, nu_ref, nv_ref, x_ref, g_ref, wg_ref, wu_ref, wd_ref, o_ref, xn_scr):
    i = pl.program_id(0)
    f = pl.program_id(1)
    used = i < nu_ref[0]
    half = o_ref.shape[0] // 2

    @pl.when(f == 0)
    def _():
        o_ref[...] = jnp.zeros_like(o_ref)

    @pl.when(jnp.logical_and(used, f == 0))
    def _():
        xn_scr[...] = _rms(x_ref[...], g_ref[...]).astype(BF16)

    def swiglu_rows(nrows):
        xn = xn_scr[:nrows, :]
        gg = _dot(xn, wg_ref[0, 0].astype(BF16))
        uu = _dot(xn, wu_ref[0, 0].astype(BF16))
        a = (gg * jax.nn.sigmoid(gg) * uu).astype(BF16)
        o_ref[:nrows, :] += _dot(a, wd_ref[0, 0].astype(BF16))

    @pl.when(jnp.logical_and(used, nv_ref[i] > half))
    def _():
        swiglu_rows(2 * half)

    @pl.when(jnp.logical_and(used, nv_ref[i] <= half))
    def _():
        swiglu_rows(half)


def _combine_kernel(cur_ref, nxt_ref, h_ref, gate_ref, y_hbm, og_ref, o_ref, buf, sems, *, final):
    i = pl.program_id(0)
    nstep = pl.num_programs(0)
    rows = h_ref.shape[0]

    def row_copy(src_row, par, k, r):
        return pltpu.make_async_copy(y_hbm.at[pl.ds(src_row, 1), :], buf.at[par, k, pl.ds(r, 1), :], sems.at[par])

    def issue(idx_ref, par):
        for r in range(rows):
            for k in range(TOP_K):
                row_copy(idx_ref[0, 0, TOP_K * r + k], par, k, r).start()

    def drain(par):
        for r in range(rows):
            for k in range(TOP_K):
                row_copy(0, par, k, r).wait()

    @pl.when(i == 0)
    def _():
        issue(cur_ref, 0)

    for par in range(2):
        @pl.when(i % 2 == par)
        def _():
            @pl.when(i + 1 < nstep)
            def _():
                issue(nxt_ref, 1 - par)

            drain(par)
            gate = gate_ref[...]
            out = h_ref[...] + gate[:, 0:1] * buf[par, 0] + gate[:, 1:2] * buf[par, 1]
            o_ref[...] = _rms(out, og_ref[...]) if final else out


def _moe(h, g, layer, w_router, w_gu, w_down, out_g, final, *, tm=1024, tf=512, rows=512, drows=1024, zrows=256):
    n, d = h.shape
    ne = w_router.shape[1]
    f = w_down.shape[2]
    nf = f // tf
    row = lambda i: (i, 0)
    idx, gate = pl.pallas_call(
        _route_kernel,
        grid=(n // 512,),
        in_specs=[pl.BlockSpec((512, d), row), _resident((1, d)), _resident((ne, d))],
        out_specs=[pl.BlockSpec((TOP_K, 512), lambda i: (0, i)), pl.BlockSpec((512, LANES), row)],
        out_shape=[jax.ShapeDtypeStruct((TOP_K, n), jnp.int32), jax.ShapeDtypeStruct((n, LANES), F32)],
        compiler_params=_cparams("parallel"),
        name="moe_route",
    )(h, g.reshape(1, d), w_router.T)

    npair = n * TOP_K
    ntile = npair // tm + ne
    e_flat = idx.T.reshape(npair)
    onehot = (e_flat[:, None] == jnp.arange(ne, dtype=jnp.int32)[None, :]).astype(jnp.int32)
    csum = jnp.cumsum(onehot, axis=0)
    rank = jnp.sum((csum - onehot) * onehot, axis=1)
    counts = csum[-1]
    padded = ((counts + tm - 1) // tm) * tm
    ends = jnp.cumsum(padded)
    dest = (ends - padded)[e_flat] + rank
    tile_start = jnp.arange(ntile, dtype=jnp.int32) * tm
    spare = ends[-1] + tile_start[:ne]
    pad_tile = jnp.concatenate([jnp.where(padded > 0, ends - tm, -1),
                                jnp.where(spare < ntile * tm, spare, -1)]).astype(jnp.int32)
    tile_expert = jnp.minimum(jnp.sum((tile_start[:, None] >= ends[None, :]).astype(jnp.int32), axis=1), ne - 1)
    n_used = (ends[-1] // tm).astype(jnp.int32).reshape(1)
    n_valid = jnp.clip(counts[tile_expert] - (tile_start - (ends - padded)[tile_expert]), 0, tm).astype(jnp.int32)

    nblk = n // rows
    pair_blk = lambda r: pl.BlockSpec((1, 1, TOP_K * r), lambda i: (i, 0, 0), memory_space=pltpu.SMEM)
    xs = pl.pallas_call(
        functools.partial(_dispatch_kernel, tm=tm),
        grid=(n // drows,),
        in_specs=[pair_blk(drows), pl.BlockSpec(memory_space=pltpu.SMEM), pl.BlockSpec((drows, d), row)],
        out_specs=pl.BlockSpec(memory_space=pl.ANY),
        out_shape=jax.ShapeDtypeStruct((ntile * tm, d), F32),
        scratch_shapes=[pltpu.VMEM((zrows, d), F32), pltpu.SemaphoreType.DMA(()), pltpu.SemaphoreType.DMA(())],
        compiler_params=_cparams("arbitrary"),
        name="moe_dispatch",
    )(dest.reshape(n // drows, 1, TOP_K * drows), pad_tile, h)
    dest = dest.reshape(nblk, 1, TOP_K * rows)

    def wsel(off):
        def index(i, j, te, nu, nv):
            return (layer, te[i], 0, off + jnp.where(i < nu[0], j, nf - 1))
        return index

    def wdsel(i, j, te, nu, nv):
        return (layer, te[i], jnp.where(i < nu[0], j, nf - 1), 0)

    ys = pl.pallas_call(
        _expert_kernel,
        grid_spec=pltpu.PrefetchScalarGridSpec(
            num_scalar_prefetch=3,
            grid=(ntile, nf),
            in_specs=[pl.BlockSpec((tm, d), lambda i, j, te, nu, nv: (jnp.where(i < nu[0], i, 0), 0)),
                      pl.BlockSpec((1, d), lambda i, j, te, nu, nv: (0, 0)),
                      pl.BlockSpec((1, 1, d, tf), wsel(0)),
                      pl.BlockSpec((1, 1, d, tf), wsel(nf)),
                      pl.BlockSpec((1, 1, tf, d), wdsel)],
            out_specs=pl.BlockSpec((tm, d), lambda i, j, te, nu, nv: (i, 0)),
            scratch_shapes=[pltpu.VMEM((tm, d), BF16)],
        ),
        out_shape=jax.ShapeDtypeStruct((ntile * tm, d), F32),
        compiler_params=_cparams("arbitrary", "arbitrary"),
        name="moe_expert",
    )(tile_expert, n_used, n_valid, xs, g.reshape(1, d), w_gu, w_gu, w_down)

    return pl.pallas_call(
        functools.partial(_combine_kernel, final=final),
        grid=(nblk,),
        in_specs=[pair_blk(rows),
                  pl.BlockSpec((1, 1, TOP_K * rows), lambda i: (jnp.minimum(i + 1, nblk - 1), 0, 0),
                               memory_space=pltpu.SMEM),
                  pl.BlockSpec((rows, d), row), pl.BlockSpec((rows, LANES), row),
                  pl.BlockSpec(memory_space=pl.ANY), _resident((1, d))],
        out_specs=pl.BlockSpec((rows, d), row),
        out_shape=jax.ShapeDtypeStruct((n, d), F32),
        scratch_shapes=[pltpu.VMEM((2, TOP_K, rows, d), F32), pltpu.SemaphoreType.DMA((2,))],
        compiler_params=_cparams("arbitrary"),
        name="moe_combine",
    )(dest, dest, h, gate, ys, out_g.reshape(1, d))


def _to_time_major(x_ref, slab):
    nb, tt, d = x_ref.shape
    for b in range(nb):
        for j in range(d // LANES):
            slab[j, pl.ds(b, tt, stride=nb), :] = x_ref[b, :, j * LANES:(j + 1) * LANES]
    return jnp.concatenate([slab[j] for j in range(d // LANES)], axis=1)


def _residual_from_time_major(y, x_ref, o_ref, slab):
    nb, tt, d = x_ref.shape
    for j in range(d // LANES):
        slab[j] = y[:, j * LANES:(j + 1) * LANES]
    for b in range(nb):
        for j in range(d // LANES):
            cs = slice(j * LANES, (j + 1) * LANES)
            o_ref[b, :, cs] = x_ref[b, :, cs] + slab[j, pl.ds(b, tt, stride=nb), :]


def _s5_prep_kernel(ar_ref, ai_ref, ld_ref, brt_ref, bit_ref, bbr_ref, bbi_ref, lr_ref, li_ref):
    ar = ar_ref[...]
    ai = ai_ref[...]
    dt = jnp.exp(ld_ref[...])
    mag = jnp.exp(ar * dt)
    lr = mag * jnp.cos(ai * dt)
    li = mag * jnp.sin(ai * dt)
    lr_ref[...] = lr
    li_ref[...] = li
    den = ar * ar + ai * ai
    nr = lr - 1.0
    kr = ((nr * ar + li * ai) / den)[:, None, :]
    ki = ((li * ar - nr * ai) / den)[:, None, :]
    brt = brt_ref[...]
    bit = bit_ref[...]
    bbr_ref[...] = kr * brt - ki * bit
    bbi_ref[...] = kr * bit + ki * brt


def _s5_kernel(h_ref, g_ref, bre_ref, bim_ref, cre_ref, cim_ref, lr_ref, li_ref, d_ref, w_ref, o_ref,
               slab_in, slab_out, s_re, s_im, c_re, c_im):
    nb, tt, d = h_ref.shape
    nslab, _, sw = bre_ref.shape

    @pl.when(pl.program_id(0) == 0)
    def _():
        c_re[...] = jnp.zeros_like(c_re)
        c_im[...] = jnp.zeros_like(c_im)

    hn = _rms(_to_time_major(h_ref, slab_in), g_ref[...])
    hb = hn.astype(BF16)
    for j in range(nslab):
        xj = hb[:, j * LANES:(j + 1) * LANES]
        s_re[j] = _dot(xj, bre_ref[j])
        s_im[j] = _dot(xj, bim_ref[j])

    ys = []
    for j in range(nslab):
        lr = jnp.broadcast_to(lr_ref[j:j + 1, :], (nb, sw))
        li = jnp.broadcast_to(li_ref[j:j + 1, :], (nb, sw))

        def step(t, carry, j=j, lr=lr, li=li):
            cr, ci = carry
            rows = pl.ds(pl.multiple_of(t * nb, nb), nb)
            nr = lr * cr - li * ci + s_re[j, rows, :]
            ni = lr * ci + li * cr + s_im[j, rows, :]
            s_re[j, rows, :] = nr
            s_im[j, rows, :] = ni
            return nr, ni

        cr, ci = lax.fori_loop(0, tt, step, (c_re[j], c_im[j]), unroll=True)
        c_re[j] = cr
        c_im[j] = ci
        ys.append(_dot(s_re[j].astype(BF16), cre_ref[j]) - _dot(s_im[j].astype(BF16), cim_ref[j]))

    y = jnp.concatenate(ys, axis=1) + d_ref[...] * hn
    ab = _dot(_gelu(y).astype(BF16), w_ref[...])
    _residual_from_time_major(ab[:, :d] * jax.nn.sigmoid(ab[:, d:]), h_ref, o_ref, slab_out)


def _block_diag(blocks, per):
    n, a, b = blocks.shape
    eye = jnp.eye(per, dtype=blocks.dtype)
    out = blocks.reshape(n // per, per, a, 1, b) * eye[None, :, None, :, None]
    return out.reshape(n // per, per * a, per * b)


def _mixer_s5(h, g, bsz, a_re, a_im, log_dt, b_re, b_im, c_re, c_im, d_skip, w_glu, *, tt=64):
    n, d = h.shape
    seq = n // bsz
    ng, p = a_re.shape
    per = LANES // S5_GROUP
    nslab = ng // per
    sw = per * p
    full = lambda *shape: pl.BlockSpec(shape, lambda i: (0,) * len(shape))
    bbr, bbi, lr, li = pl.pallas_call(
        _s5_prep_kernel,
        grid=(1,),
        in_specs=[full(ng, p)] * 3 + [full(ng, S5_GROUP, p)] * 2,
        out_specs=[full(ng, S5_GROUP, p)] * 2 + [full(ng, p)] * 2,
        out_shape=[jax.ShapeDtypeStruct((ng, S5_GROUP, p), F32)] * 2 + [jax.ShapeDtypeStruct((ng, p), F32)] * 2,
        name="s5_prep",
    )(a_re, a_im, jnp.broadcast_to(log_dt[:, None], (ng, p)), jnp.swapaxes(b_re, 1, 2), jnp.swapaxes(b_im, 1, 2))
    bre = _block_diag(bbr, per).astype(BF16)
    bim = _block_diag(bbi, per).astype(BF16)
    cre = _block_diag(jnp.swapaxes(c_re, 1, 2), per).astype(BF16)
    cim = _block_diag(jnp.swapaxes(c_im, 1, 2), per).astype(BF16)
    blk = pl.BlockSpec((bsz, tt, d), lambda i: (0, i, 0))
    out = pl.pallas_call(
        _s5_kernel,
        grid=(seq // tt,),
        in_specs=[blk, _resident((1, d)), _resident((nslab, LANES, sw)), _resident((nslab, LANES, sw)),
                  _resident((nslab, sw, LANES)), _resident((nslab, sw, LANES)), _resident((nslab, sw)),
                  _resident((nslab, sw)), _resident((1, d)), _resident((d, 2 * d))],
        out_specs=blk,
        out_shape=jax.ShapeDtypeStruct((bsz, seq, d), F32),
        scratch_shapes=[pltpu.VMEM((d // LANES, tt * bsz, LANES), F32)] * 2
        + [pltpu.VMEM((nslab, tt * bsz, sw), F32)] * 2 + [pltpu.VMEM((nslab, bsz, sw), F32)] * 2,
        compiler_params=_cparams("arbitrary"),
        name="s5_mixer",
    )(h.reshape(bsz, seq, d), g.reshape(1, d), bre, bim, cre, cim, lr.reshape(nslab, sw), li.reshape(nslab, sw),
      d_skip.reshape(1, d), w_glu.astype(BF16))
    return out.reshape(n, d)


def _rglru_kernel(h_ref, g_ref, win_ref, cw_ref, cb_ref, wr_ref, br_ref, wi_ref, bi_ref, lam_ref, wout_ref, o_ref,
                  slab_in, slab_out, hist, state, a_scr, b_scr):
    nb, tt, d = h_ref.shape
    r = wout_ref.shape[0]
    kw = cw_ref.shape[0]
    nh = wr_ref.shape[0]
    hd = r // nh
    rows = tt * nb

    @pl.when(pl.program_id(0) == 0)
    def _():
        hist[...] = jnp.zeros_like(hist)
        state[...] = jnp.zeros_like(state)

    hn = _rms(_to_time_major(h_ref, slab_in), g_ref[...]).astype(BF16)
    xb = _dot(hn, win_ref[:, :r])
    gate = _gelu(_dot(hn, win_ref[:, r:]))
    xpad = jnp.concatenate([hist[...], xb], axis=0)
    hist[...] = xb[rows - (kw - 1) * nb:]
    xc = cb_ref[...]
    for k in range(kw):
        xc = xc + cw_ref[k:k + 1, :] * xpad[k * nb:k * nb + rows]
    lam = -lam_ref[...]
    sp = jnp.maximum(lam, 0.0) + jnp.log1p(jnp.exp(-jnp.abs(lam)))
    for hh in range(nh):
        cs = slice(hh * hd, (hh + 1) * hd)
        xh = xc[:, cs]
        xhb = xh.astype(BF16)
        rg = jax.nn.sigmoid(_dot(xhb, wr_ref[hh]) + br_ref[:, cs])
        ig = jax.nn.sigmoid(_dot(xhb, wi_ref[hh]) + bi_ref[:, cs])
        a = jnp.exp((-LRU_C) * rg * sp[:, cs])
        a_scr[:, cs] = a
        b_scr[:, cs] = jnp.sqrt(1.0 - a * a) * (ig * xh)

    def step(t, hcur):
        sl = pl.ds(pl.multiple_of(t * nb, nb), nb)
        hnew = a_scr[sl, :] * hcur + b_scr[sl, :]
        b_scr[sl, :] = hnew
        return hnew

    state[...] = lax.fori_loop(0, tt, step, state[...], unroll=True)
    y = (b_scr[...] * gate).astype(BF16)
    _residual_from_time_major(_dot(y, wout_ref[...]), h_ref, o_ref, slab_out)


def _mixer_rglru(h, g, bsz, w_in, conv_w, conv_b, w_r, b_r, w_i, b_i, lam, w_out, *, tt=64):
    n, d = h.shape
    seq = n // bsz
    r = w_out.shape[0]
    nh, hd, _ = w_r.shape
    kw = conv_w.shape[0]
    blk = pl.BlockSpec((bsz, tt, d), lambda i: (0, i, 0))
    out = pl.pallas_call(
        _rglru_kernel,
        grid=(seq // tt,),
        in_specs=[blk, _resident((1, d)), _resident((d, 2 * r)), _resident((kw, r)), _resident((1, r)),
                  _resident((nh, hd, hd)), _resident((1, r)), _resident((nh, hd, hd)), _resident((1, r)),
                  _resident((1, r)), _resident((r, d))],
        out_specs=blk,
        out_shape=jax.ShapeDtypeStruct((bsz, seq, d), F32),
        scratch_shapes=[pltpu.VMEM((d // LANES, tt * bsz, LANES), F32)] * 2
        + [pltpu.VMEM(((kw - 1) * bsz, r), F32), pltpu.VMEM((bsz, r), F32),
           pltpu.VMEM((tt * bsz, r), F32), pltpu.VMEM((tt * bsz, r), F32)],
        compiler_params=_cparams("arbitrary"),
        name="rglru_mixer",
    )(h.reshape(bsz, seq, d), g.reshape(1, d), w_in.astype(BF16), conv_w, conv_b.reshape(1, r), w_r.astype(BF16),
      b_r.reshape(1, r), w_i.astype(BF16), b_i.reshape(1, r), lam.reshape(1, r), w_out.astype(BF16))
    return out.reshape(n, d)


def kernel(x, norm_mix_g, norm_ffn_g, norm_final_g, a_w_in, a_ln_g, a_ln_b, a_w_s, a_b_s, a_w_out, b_a_re, b_a_im, b_log_dt, b_b_re, b_b_im, b_c_re, b_c_im, b_d, b_w_glu, c_w_in, c_conv_w, c_conv_b, c_w_r, c_b_r, c_w_i, c_b_i, c_lam, c_w_out, f_w_gu, f_w_down, m_w_router, m_w_gu, m_w_down):
    bsz, seq, d = x.shape
    depth = norm_mix_g.shape[0]
    h = x.reshape(bsz * seq, d)
    for i in range(depth):
        kind = i % N_MIXERS
        j = i // N_MIXERS
        if kind == 0:
            h = _mixer_gmlp(h, norm_mix_g[i], a_w_in[j], a_ln_g[j], a_ln_b[j], a_w_s[j], a_b_s[j], a_w_out[j])
        elif kind == 1:
            h = _mixer_s5(h, norm_mix_g[i], bsz, b_a_re[j], b_a_im[j], b_log_dt[j], b_b_re[j], b_b_im[j],
                          b_c_re[j], b_c_im[j], b_d[j], b_w_glu[j])
        else:
            h = _mixer_rglru(h, norm_mix_g[i], bsz, c_w_in[j], c_conv_w[j], c_conv_b[j], c_w_r[j], c_b_r[j],
                             c_w_i[j], c_b_i[j], c_lam[j], c_w_out[j])
        k = i // 2
        final = i == depth - 1
        if i % 2 == 0:
            h = _ffn_dense(h, norm_ffn_g[i], f_w_gu[k], f_w_down[k], norm_final_g, final)
        else:
            h = _moe(h, norm_ffn_g[i], k, m_w_router[k], m_w_gu, m_w_down, norm_final_g, final)
    return h.reshape(bsz, seq, d)
```

```python
import functools
import math

import jax
import jax.numpy as jnp
from jax import lax
from jax.experimental import pallas as pl
from jax.experimental.pallas import tpu as pltpu

F32 = jnp.float32
BF16 = jnp.bfloat16
EPS = 1e-6
N_MIXERS = 3
CHUNK = 128
N_GROUPS_A = 8
S5_GROUP = 16
N_HEADS_C = 12
LRU_C = 8.0
TOP_K = 2
LANES = 128
MXU_TILE = 256
VMEM_LIMIT = 56 * 2**20


def _cparams(*sem):
    return pltpu.CompilerParams(dimension_semantics=sem, vmem_limit_bytes=VMEM_LIMIT)


def _resident(shape):
    nd = len(shape)
    return pl.BlockSpec(shape, lambda *_: (0,) * nd, pipeline_mode=pl.Buffered(1))


def _rms(x, g):
    return x * lax.rsqrt(jnp.mean(x * x, axis=-1, keepdims=True) + EPS) * g


def _gelu(x):
    c = math.sqrt(2.0 / math.pi)
    return (0.5 * x) * (1.0 + jnp.tanh(x * (c + (c * 0.044715) * (x * x))))


def _dot(a, b):
    return jnp.dot(a, b, preferred_element_type=F32)


def _mxu_chunks(size, n):
    tiles = size // MXU_TILE
    assert tiles * MXU_TILE == size
    return [MXU_TILE * (tiles // n + (1 if i < tiles % n else 0)) for i in range(n)]


def _gmlp_in_kernel(h_ref, g_ref, w_ref, lng_ref, lnb_ref, u_ref, v_ref):
    e = u_ref.shape[-1]
    hn = _rms(h_ref[...], g_ref[...]).astype(BF16)
    v = _gelu(_dot(hn, w_ref[:, e:]))
    vc = v - jnp.mean(v, axis=-1, keepdims=True)
    var = jnp.mean(vc * vc, axis=-1, keepdims=True)
    v_ref[...] = (vc * lax.rsqrt(var + EPS) * lng_ref[...] + lnb_ref[...]).astype(BF16)
    u_ref[...] = _gelu(_dot(hn, w_ref[:, :e])).astype(BF16)


def _gmlp_out_kernel(u_ref, v_ref, ws_ref, bs_ref, wo_ref, h_ref, o_ref, y_scr):
    tc, e = u_ref.shape
    ng = ws_ref.shape[0]
    ge = e // ng
    for c in range(tc // CHUNK):
        r = slice(c * CHUNK, (c + 1) * CHUNK)
        for g in range(ng):
            cs = slice(g * ge, (g + 1) * ge)
            sv = _dot(ws_ref[g], v_ref[r, cs]) + bs_ref[:, g:g + 1]
            y_scr[r, cs] = (u_ref[r, cs].astype(F32) * sv).astype(BF16)
    o_ref[...] = h_ref[...] + _dot(y_scr[...], wo_ref[...])


def _mixer_gmlp(h, g, w_in, ln_g, ln_b, w_s, b_s, w_out, *, tm=512):
    n, d = h.shape
    e = w_out.shape[0]
    ng = w_s.shape[0]
    row = lambda i: (i, 0)
    u, v = pl.pallas_call(
        _gmlp_in_kernel,
        grid=(n // tm,),
        in_specs=[pl.BlockSpec((tm, d), row), _resident((1, d)), _resident((d, 2 * e)),
                  _resident((1, e)), _resident((1, e))],
        out_specs=[pl.BlockSpec((tm, e), row), pl.BlockSpec((tm, e), row)],
        out_shape=[jax.ShapeDtypeStruct((n, e), BF16)] * 2,
        compiler_params=_cparams("parallel"),
        name="gmlp_in",
    )(h, g.reshape(1, d), w_in.astype(BF16), ln_g.reshape(1, e), ln_b.reshape(1, e))
    mask = jnp.tril(jnp.ones((CHUNK, CHUNK), dtype=bool))
    ws = jnp.where(mask[None], w_s, 0.0).astype(BF16)
    return pl.pallas_call(
        _gmlp_out_kernel,
        grid=(n // tm,),
        in_specs=[pl.BlockSpec((tm, e), row), pl.BlockSpec((tm, e), row), _resident((ng, CHUNK, CHUNK)),
                  _resident((CHUNK, ng)), _resident((e, d)), pl.BlockSpec((tm, d), row)],
        out_specs=pl.BlockSpec((tm, d), row),
        out_shape=jax.ShapeDtypeStruct((n, d), F32),
        scratch_shapes=[pltpu.VMEM((tm, e), BF16)],
        compiler_params=_cparams("parallel"),
        name="gmlp_out",
    )(u, v, ws, b_s.T, w_out.astype(BF16), h)


def _ffn_kernel(h_ref, g_ref, wgu_ref, wd_ref, og_ref, o_ref, *, chunks, final):
    x = h_ref[...]
    hn = _rms(x, g_ref[...]).astype(BF16)
    f = wd_ref.shape[0]
    acc = x
    f0 = 0
    for fc in chunks:
        gg = _dot(hn, wgu_ref[:, f0:f0 + fc])
        uu = _dot(hn, wgu_ref[:, f + f0:f + f0 + fc])
        a = (gg * jax.nn.sigmoid(gg) * uu).astype(BF16)
        acc = acc + _dot(a, wd_ref[f0:f0 + fc, :])
        f0 += fc
    o_ref[...] = _rms(acc, og_ref[...]) if final else acc


def _ffn_dense(h, g, w_gu, w_down, out_g, final, *, tm=512):
    n, d = h.shape
    f = w_down.shape[0]
    row = lambda i: (i, 0)
    return pl.pallas_call(
        functools.partial(_ffn_kernel, chunks=_mxu_chunks(f, 2), final=final),
        grid=(n // tm,),
        in_specs=[pl.BlockSpec((tm, d), row), _resident((1, d)), _resident((d, 2 * f)), _resident((f, d)),
                  _resident((1, d))],
        out_specs=pl.BlockSpec((tm, d), row),
        out_shape=jax.ShapeDtypeStruct((n, d), F32),
        compiler_params=_cparams("parallel"),
        name="ffn_dense",
    )(h, g.reshape(1, d), w_gu.astype(BF16), w_down.astype(BF16), out_g.reshape(1, d))


def _route_kernel(h_ref, g_ref, wrt_ref, idx_ref, gate_ref):
    hn = _rms(h_ref[...], g_ref[...])
    logits = lax.dot_general(wrt_ref[...], hn, (((1,), (1,)), ((), ())), preferred_element_type=F32,
                             precision=lax.Precision.HIGHEST)
    ne = logits.shape[0]
    eid = lax.broadcasted_iota(jnp.int32, logits.shape, 0)
    neg = jnp.float32(-jnp.inf)
    m1 = jnp.max(logits, axis=0, keepdims=True)
    i1 = jnp.min(jnp.where(logits == m1, eid, ne), axis=0, keepdims=True)
    lg2 = jnp.where(eid == i1, neg, logits)
    m2 = jnp.max(lg2, axis=0, keepdims=True)
    i2 = jnp.min(jnp.where(lg2 == m2, eid, ne), axis=0, keepdims=True)
    ex = jnp.exp(m2 - m1)
    idx_ref[...] = jnp.concatenate([i1, i2], axis=0)
    gates = jnp.concatenate([1.0 / (1.0 + ex), ex / (1.0 + ex), jnp.zeros((LANES - TOP_K, ex.shape[1]), F32)], axis=0)
    gate_ref[...] = gates.T


def _dispatch_kernel(dst_ref, pad_ref, x_ref, o_hbm, zbuf, sem, zsem, *, tm):
    i = pl.program_id(0)
    rows = x_ref.shape[0]
    zrows = zbuf.shape[0]

    @pl.when(i == 0)
    def _():
        zbuf[...] = jnp.zeros_like(zbuf)
        for start_wait in (True, False):
            for e in range(pad_ref.shape[0]):
                first = pad_ref[e]

                @pl.when(first >= 0)
                def _():
                    base = pl.multiple_of(jnp.maximum(first, 0), zrows)
                    for q in range(tm // zrows):
                        cp = pltpu.make_async_copy(zbuf, o_hbm.at[pl.ds(base + q * zrows, zrows), :], zsem)
                        cp.start() if start_wait else cp.wait()

    def row_copy(r, slot_row):
        return pltpu.make_async_copy(x_ref.at[pl.ds(r, 1), :], o_hbm.at[pl.ds(slot_row, 1), :], sem)

    for r in range(rows):
        for k in range(TOP_K):
            row_copy(r, dst_ref[0, 0, TOP_K * r + k]).start()
    for r in range(rows):
        for k in range(TOP_K):
            row_copy(r, 0).wait()


def _expert_kernel(te_ref, nu_ref, nv_ref, x_ref, g_ref, wg_ref, wu_ref, wd_ref, o_ref, xn_scr):
    i = pl.program_id(0)
    f = pl.program_id(1)
    used = i < nu_ref[0]
    half = o_ref.shape[0] // 2

    @pl.when(f == 0)
    def _():
        o_ref[...] = jnp.zeros_like(o_ref)

    @pl.when(jnp.logical_and(used, f == 0))
    def _():
        xn_scr[...] = _rms(x_ref[...], g_ref[...]).astype(BF16)

    def swiglu_rows(nrows):
        xn = xn_scr[:nrows, :]
        gg = _dot(xn, wg_ref[0, 0].astype(BF16))
        uu = _dot(xn, wu_ref[0, 0].astype(BF16))
        a = (gg * jax.nn.sigmoid(gg) * uu).astype(BF16)
        o_ref[:nrows, :] += _dot(a, wd_ref[0, 0].astype(BF16))

    @pl.when(jnp.logical_and(used, nv_ref[i] > half))
    def _():
        swiglu_rows(2 * half)

    @pl.when(jnp.logical_and(used, nv_ref[i] <= half))
    def _():
        swiglu_rows(half)


def _combine_kernel(cur_ref, nxt_ref, h_ref, gate_ref, y_hbm, og_ref, o_ref, buf, sems, *, final):
    i = pl.program_id(0)
    nstep = pl.num_programs(0)
    rows = h_ref.shape[0]

    def row_copy(src_row, par, k, r):
        return pltpu.make_async_copy(y_hbm.at[pl.ds(src_row, 1), :], buf.at[par, k, pl.ds(r, 1), :], sems.at[par])

    def issue(idx_ref, par):
        for r in range(rows):
            for k in range(TOP_K):
                row_copy(idx_ref[0, 0, TOP_K * r + k], par, k, r).start()

    def drain(par):
        for r in range(rows):
            for k in range(TOP_K):
                row_copy(0, par, k, r).wait()

    @pl.when(i == 0)
    def _():
        issue(cur_ref, 0)

    for par in range(2):
        @pl.when(i % 2 == par)
        def _():
            @pl.when(i + 1 < nstep)
            def _():
                issue(nxt_ref, 1 - par)

            drain(par)
            gate = gate_ref[...]
            out = h_ref[...] + gate[:, 0:1] * buf[par, 0] + gate[:, 1:2] * buf[par, 1]
            o_ref[...] = _rms(out, og_ref[...]) if final else out


def _moe(h, g, layer, w_router, w_gu, w_down, out_g, final, *, tm=1024, tf=512, rows=256, drows=1024, zrows=256):
    n, d = h.shape
    ne = w_router.shape[1]
    f = w_down.shape[2]
    nf = f // tf
    row = lambda i: (i, 0)
    idx, gate = pl.pallas_call(
        _route_kernel,
        grid=(n // 512,),
        in_specs=[pl.BlockSpec((512, d), row), _resident((1, d)), _resident((ne, d))],
        out_specs=[pl.BlockSpec((TOP_K, 512), lambda i: (0, i)), pl.BlockSpec((512, LANES), row)],
        out_shape=[jax.ShapeDtypeStruct((TOP_K, n), jnp.int32), jax.ShapeDtypeStruct((n, LANES), F32)],
        compiler_params=_cparams("parallel"),
        name="moe_route",
    )(h, g.reshape(1, d), w_router.T)

    npair = n * TOP_K
    ntile = npair // tm + ne
    e_flat = idx.T.reshape(npair)
    onehot = (e_flat[:, None] == jnp.arange(ne, dtype=jnp.int32)[None, :]).astype(jnp.int32)
    csum = jnp.cumsum(onehot, axis=0)
    rank = jnp.sum((csum - onehot) * onehot, axis=1)
    counts = csum[-1]
    padded = ((counts + tm - 1) // tm) * tm
    ends = jnp.cumsum(padded)
    dest = (ends - padded)[e_flat] + rank
    tile_start = jnp.arange(ntile, dtype=jnp.int32) * tm
    spare = ends[-1] + tile_start[:ne]
    pad_tile = jnp.concatenate([jnp.where(padded > 0, ends - tm, -1),
                                jnp.where(spare < ntile * tm, spare, -1)]).astype(jnp.int32)
    tile_expert = jnp.minimum(jnp.sum((tile_start[:, None] >= ends[None, :]).astype(jnp.int32), axis=1), ne - 1)
    n_used = (ends[-1] // tm).astype(jnp.int32).reshape(1)
    in_group = tile_expert[:, None] == jnp.arange(ne, dtype=jnp.int32)[None, :]
    last_row = jnp.sum(jnp.where(in_group, (ends - padded + counts)[None, :], 0), axis=1)
    n_valid = jnp.clip(last_row - tile_start, 0, tm).astype(jnp.int32)

    nblk = n // rows
    pair_blk = lambda r: pl.BlockSpec((1, 1, TOP_K * r), lambda i: (i, 0, 0), memory_space=pltpu.SMEM)
    xs = pl.pallas_call(
        functools.partial(_dispatch_kernel, tm=tm),
        grid=(n // drows,),
        in_specs=[pair_blk(drows), pl.BlockSpec(memory_space=pltpu.SMEM), pl.BlockSpec((drows, d), row)],
        out_specs=pl.BlockSpec(memory_space=pl.ANY),
        out_shape=jax.ShapeDtypeStruct((ntile * tm, d), F32),
        scratch_shapes=[pltpu.VMEM((zrows, d), F32), pltpu.SemaphoreType.DMA(()), pltpu.SemaphoreType.DMA(())],
        compiler_params=_cparams("arbitrary"),
        name="moe_dispatch",
    )(dest.reshape(n // drows, 1, TOP_K * drows), pad_tile, h)
    dest = dest.reshape(nblk, 1, TOP_K * rows)

    def wsel(off):
        def index(i, j, te, nu, nv):
            return (layer, te[i], 0, off + jnp.where(i < nu[0], j, nf - 1))
        return index

    def wdsel(i, j, te, nu, nv):
        return (layer, te[i], jnp.where(i < nu[0], j, nf - 1), 0)

    ys = pl.pallas_call(
        _expert_kernel,
        grid_spec=pltpu.PrefetchScalarGridSpec(
            num_scalar_prefetch=3,
            grid=(ntile, nf),
            in_specs=[pl.BlockSpec((tm, d), lambda i, j, te, nu, nv: (jnp.where(i < nu[0], i, 0), 0)),
                      pl.BlockSpec((1, d), lambda i, j, te, nu, nv: (0, 0)),
                      pl.BlockSpec((1, 1, d, tf), wsel(0)),
                      pl.BlockSpec((1, 1, d, tf), wsel(nf)),
                      pl.BlockSpec((1, 1, tf, d), wdsel)],
            out_specs=pl.BlockSpec((tm, d), lambda i, j, te, nu, nv: (i, 0)),
            scratch_shapes=[pltpu.VMEM((tm, d), BF16)],
        ),
        out_shape=jax.ShapeDtypeStruct((ntile * tm, d), F32),
        compiler_params=_cparams("arbitrary", "arbitrary"),
        name="moe_expert",
    )(tile_expert, n_used, n_valid, xs, g.reshape(1, d), w_gu, w_gu, w_down)

    return pl.pallas_call(
        functools.partial(_combine_kernel, final=final),
        grid=(nblk,),
        in_specs=[pair_blk(rows),
                  pl.BlockSpec((1, 1, TOP_K * rows), lambda i: (jnp.minimum(i + 1, nblk - 1), 0, 0),
                               memory_space=pltpu.SMEM),
                  pl.BlockSpec((rows, d), row), pl.BlockSpec((rows, LANES), row),
                  pl.BlockSpec(memory_space=pl.ANY), _resident((1, d))],
        out_specs=pl.BlockSpec((rows, d), row),
        out_shape=jax.ShapeDtypeStruct((n, d), F32),
        scratch_shapes=[pltpu.VMEM((2, TOP_K, rows, d), F32), pltpu.SemaphoreType.DMA((2,))],
        compiler_params=_cparams("arbitrary"),
        name="moe_combine",
    )(dest, dest, h, gate, ys, out_g.reshape(1, d))


def _to_time_major(x_ref, slab):
    nb, tt, d = x_ref.shape
    for b in range(nb):
        for j in range(d // LANES):
            slab[j, pl.ds(b, tt, stride=nb), :] = x_ref[b, :, j * LANES:(j + 1) * LANES]
    return jnp.concatenate([slab[j] for j in range(d // LANES)], axis=1)


def _residual_from_time_major(y, x_ref, o_ref, slab):
    nb, tt, d = x_ref.shape
    for j in range(d // LANES):
        slab[j] = y[:, j * LANES:(j + 1) * LANES]
    for b in range(nb):
        for j in range(d // LANES):
            cs = slice(j * LANES, (j + 1) * LANES)
            o_ref[b, :, cs] = x_ref[b, :, cs] + slab[j, pl.ds(b, tt, stride=nb), :]


def _s5_prep_kernel(ar_ref, ai_ref, ld_ref, brt_ref, bit_ref, bbr_ref, bbi_ref, lr_ref, li_ref):
    ar = ar_ref[...]
    ai = ai_ref[...]
    dt = jnp.exp(ld_ref[...])
    mag = jnp.exp(ar * dt)
    lr = mag * jnp.cos(ai * dt)
    li = mag * jnp.sin(ai * dt)
    lr_ref[...] = lr
    li_ref[...] = li
    den = ar * ar + ai * ai
    nr = lr - 1.0
    kr = ((nr * ar + li * ai) / den)[:, None, :]
    ki = ((li * ar - nr * ai) / den)[:, None, :]
    brt = brt_ref[...]
    bit = bit_ref[...]
    bbr_ref[...] = kr * brt - ki * bit
    bbi_ref[...] = kr * bit + ki * brt


def _s5_kernel(h_ref, g_ref, bre_ref, bim_ref, cre_ref, cim_ref, lr_ref, li_ref, d_ref, w_ref, o_ref,
               slab_in, slab_out, s_re, s_im, c_re, c_im):
    nb, tt, d = h_ref.shape
    nslab, _, sw = bre_ref.shape

    @pl.when(pl.program_id(0) == 0)
    def _():
        c_re[...] = jnp.zeros_like(c_re)
        c_im[...] = jnp.zeros_like(c_im)

    hn = _rms(_to_time_major(h_ref, slab_in), g_ref[...])
    hb = hn.astype(BF16)
    for j in range(nslab):
        xj = hb[:, j * LANES:(j + 1) * LANES]
        s_re[j] = _dot(xj, bre_ref[j])
        s_im[j] = _dot(xj, bim_ref[j])

    ys = []
    for j in range(nslab):
        lr = jnp.broadcast_to(lr_ref[j:j + 1, :], (nb, sw))
        li = jnp.broadcast_to(li_ref[j:j + 1, :], (nb, sw))

        def step(t, carry, j=j, lr=lr, li=li):
            cr, ci = carry
            rows = pl.ds(pl.multiple_of(t * nb, nb), nb)
            nr = lr * cr - li * ci + s_re[j, rows, :]
            ni = lr * ci + li * cr + s_im[j, rows, :]
            s_re[j, rows, :] = nr
            s_im[j, rows, :] = ni
            return nr, ni

        cr, ci = lax.fori_loop(0, tt, step, (c_re[j], c_im[j]), unroll=True)
        c_re[j] = cr
        c_im[j] = ci
        ys.append(_dot(s_re[j].astype(BF16), cre_ref[j]) - _dot(s_im[j].astype(BF16), cim_ref[j]))

    y = jnp.concatenate(ys, axis=1) + d_ref[...] * hn
    ab = _dot(_gelu(y).astype(BF16), w_ref[...])
    _residual_from_time_major(ab[:, :d] * jax.nn.sigmoid(ab[:, d:]), h_ref, o_ref, slab_out)


def _block_diag(blocks, per):
    n, a, b = blocks.shape
    eye = jnp.eye(per, dtype=blocks.dtype)
    out = blocks.reshape(n // per, per, a, 1, b) * eye[None, :, None, :, None]
    return out.reshape(n // per, per * a, per * b)


def _mixer_s5(h, g, bsz, a_re, a_im, log_dt, b_re, b_im, c_re, c_im, d_skip, w_glu, *, tt=64):
    n, d = h.shape
    seq = n // bsz
    ng, p = a_re.shape
    per = LANES // S5_GROUP
    nslab = ng // per
    sw = per * p
    full = lambda *shape: pl.BlockSpec(shape, lambda i: (0,) * len(shape))
    bbr, bbi, lr, li = pl.pallas_call(
        _s5_prep_kernel,
        grid=(1,),
        in_specs=[full(ng, p)] * 3 + [full(ng, S5_GROUP, p)] * 2,
        out_specs=[full(ng, S5_GROUP, p)] * 2 + [full(ng, p)] * 2,
        out_shape=[jax.ShapeDtypeStruct((ng, S5_GROUP, p), F32)] * 2 + [jax.ShapeDtypeStruct((ng, p), F32)] * 2,
        name="s5_prep",
    )(a_re, a_im, jnp.broadcast_to(log_dt[:, None], (ng, p)), jnp.swapaxes(b_re, 1, 2), jnp.swapaxes(b_im, 1, 2))
    bre = _block_diag(bbr, per).astype(BF16)
    bim = _block_diag(bbi, per).astype(BF16)
    cre = _block_diag(jnp.swapaxes(c_re, 1, 2), per).astype(BF16)
    cim = _block_diag(jnp.swapaxes(c_im, 1, 2), per).astype(BF16)
    blk = pl.BlockSpec((bsz, tt, d), lambda i: (0, i, 0))
    out = pl.pallas_call(
        _s5_kernel,
        grid=(seq // tt,),
        in_specs=[blk, _resident((1, d)), _resident((nslab, LANES, sw)), _resident((nslab, LANES, sw)),
                  _resident((nslab, sw, LANES)), _resident((nslab, sw, LANES)), _resident((nslab, sw)),
                  _resident((nslab, sw)), _resident((1, d)), _resident((d, 2 * d))],
        out_specs=blk,
        out_shape=jax.ShapeDtypeStruct((bsz, seq, d), F32),
        scratch_shapes=[pltpu.VMEM((d // LANES, tt * bsz, LANES), F32)] * 2
        + [pltpu.VMEM((nslab, tt * bsz, sw), F32)] * 2 + [pltpu.VMEM((nslab, bsz, sw), F32)] * 2,
        compiler_params=_cparams("arbitrary"),
        name="s5_mixer",
    )(h.reshape(bsz, seq, d), g.reshape(1, d), bre, bim, cre, cim, lr.reshape(nslab, sw), li.reshape(nslab, sw),
      d_skip.reshape(1, d), w_glu.astype(BF16))
    return out.reshape(n, d)


def _rglru_kernel(h_ref, g_ref, win_ref, cw_ref, cb_ref, wr_ref, br_ref, wi_ref, bi_ref, lam_ref, wout_ref, o_ref,
                  slab_in, slab_out, hist, state, a_scr, b_scr):
    nb, tt, d = h_ref.shape
    r = wout_ref.shape[0]
    kw = cw_ref.shape[0]
    nh = wr_ref.shape[0]
    hd = r // nh
    rows = tt * nb

    @pl.when(pl.program_id(0) == 0)
    def _():
        hist[...] = jnp.zeros_like(hist)
        state[...] = jnp.zeros_like(state)

    hn = _rms(_to_time_major(h_ref, slab_in), g_ref[...]).astype(BF16)
    xb = _dot(hn, win_ref[:, :r])
    gate = _gelu(_dot(hn, win_ref[:, r:]))
    xpad = jnp.concatenate([hist[...], xb], axis=0)
    hist[...] = xb[rows - (kw - 1) * nb:]
    xc = cb_ref[...]
    for k in range(kw):
        xc = xc + cw_ref[k:k + 1, :] * xpad[k * nb:k * nb + rows]
    lam = -lam_ref[...]
    sp = jnp.maximum(lam, 0.0) + jnp.log1p(jnp.exp(-jnp.abs(lam)))
    for hh in range(nh):
        cs = slice(hh * hd, (hh + 1) * hd)
        xh = xc[:, cs]
        xhb = xh.astype(BF16)
        rg = jax.nn.sigmoid(_dot(xhb, wr_ref[hh]) + br_ref[:, cs])
        ig = jax.nn.sigmoid(_dot(xhb, wi_ref[hh]) + bi_ref[:, cs])
        a = jnp.exp((-LRU_C) * rg * sp[:, cs])
        a_scr[:, cs] = a
        b_scr[:, cs] = jnp.sqrt(1.0 - a * a) * (ig * xh)

    def step(t, hcur):
        sl = pl.ds(pl.multiple_of(t * nb, nb), nb)
        hnew = a_scr[sl, :] * hcur + b_scr[sl, :]
        b_scr[sl, :] = hnew
        return hnew

    state[...] = lax.fori_loop(0, tt, step, state[...], unroll=True)
    y = (b_scr[...] * gate).astype(BF16)
    _residual_from_time_major(_dot(y, wout_ref[...]), h_ref, o_ref, slab_out)


def _mixer_rglru(h, g, bsz, w_in, conv_w, conv_b, w_r, b_r, w_i, b_i, lam, w_out, *, tt=64):
    n, d = h.shape
    seq = n // bsz
    r = w_out.shape[0]
    nh, hd, _ = w_r.shape
    kw = conv_w.shape[0]
    blk = pl.BlockSpec((bsz, tt, d), lambda i: (0, i, 0))
    out = pl.pallas_call(
        _rglru_kernel,
        grid=(seq // tt,),
        in_specs=[blk, _resident((1, d)), _resident((d, 2 * r)), _resident((kw, r)), _resident((1, r)),
                  _resident((nh, hd, hd)), _resident((1, r)), _resident((nh, hd, hd)), _resident((1, r)),
                  _resident((1, r)), _resident((r, d))],
        out_specs=blk,
        out_shape=jax.ShapeDtypeStruct((bsz, seq, d), F32),
        scratch_shapes=[pltpu.VMEM((d // LANES, tt * bsz, LANES), F32)] * 2
        + [pltpu.VMEM(((kw - 1) * bsz, r), F32), pltpu.VMEM((bsz, r), F32),
           pltpu.VMEM((tt * bsz, r), F32), pltpu.VMEM((tt * bsz, r), F32)],
        compiler_params=_cparams("arbitrary"),
        name="rglru_mixer",
    )(h.reshape(bsz, seq, d), g.reshape(1, d), w_in.astype(BF16), conv_w, conv_b.reshape(1, r), w_r.astype(BF16),
      b_r.reshape(1, r), w_i.astype(BF16), b_i.reshape(1, r), lam.reshape(1, r), w_out.astype(BF16))
    return out.reshape(n, d)


def kernel(x, norm_mix_g, norm_ffn_g, norm_final_g, a_w_in, a_ln_g, a_ln_b, a_w_s, a_b_s, a_w_out, b_a_re, b_a_im, b_log_dt, b_b_re, b_b_im, b_c_re, b_c_im, b_d, b_w_glu, c_w_in, c_conv_w, c_conv_b, c_w_r, c_b_r, c_w_i, c_b_i, c_lam, c_w_out, f_w_gu, f_w_down, m_w_router, m_w_gu, m_w_down):
    bsz, seq, d = x.shape
    depth = norm_mix_g.shape[0]
    h = x.reshape(bsz * seq, d)
    for i in range(depth):
        kind = i % N_MIXERS
        j = i // N_MIXERS
        if kind == 0:
            h = _mixer_gmlp(h, norm_mix_g[i], a_w_in[j], a_ln_g[j], a_ln_b[j], a_w_s[j], a_b_s[j], a_w_out[j])
        elif kind == 1:
            h = _mixer_s5(h, norm_mix_g[i], bsz, b_a_re[j], b_a_im[j], b_log_dt[j], b_b_re[j], b_b_im[j],
                          b_c_re[j], b_c_im[j], b_d[j], b_w_glu[j])
        else:
            h = _mixer_rglru(h, norm_mix_g[i], bsz, c_w_in[j], c_conv_w[j], c_conv_b[j], c_w_r[j], c_b_r[j],
                             c_w_i[j], c_b_i[j], c_lam[j], c_w_out[j])
        k = i // 2
        final = i == depth - 1
        if i % 2 == 0:
            h = _ffn_dense(h, norm_ffn_g[i], f_w_gu[k], f_w_down[k], norm_final_g, final)
        else:
            h = _moe(h, norm_ffn_g[i], k, m_w_router[k], m_w_gu, m_w_down, norm_final_g, final)
    return h.reshape(bsz, seq, d)
```

```python
import functools
import math

import jax
import jax.numpy as jnp
from jax import lax
from jax.experimental import pallas as pl
from jax.experimental.pallas import tpu as pltpu

F32 = jnp.float32
BF16 = jnp.bfloat16
EPS = 1e-6
N_MIXERS = 3
CHUNK = 128
N_GROUPS_A = 8
S5_GROUP = 16
N_HEADS_C = 12
LRU_C = 8.0
TOP_K = 2
LANES = 128
MXU_TILE = 256
VMEM_LIMIT = 56 * 2**20


def _cparams(*sem):
    return pltpu.CompilerParams(dimension_semantics=sem, vmem_limit_bytes=VMEM_LIMIT)


def _resident(shape):
    nd = len(shape)
    return pl.BlockSpec(shape, lambda *_: (0,) * nd, pipeline_mode=pl.Buffered(1))


def _rms(x, g):
    return x * lax.rsqrt(jnp.mean(x * x, axis=-1, keepdims=True) + EPS) * g


def _gelu(x):
    c = math.sqrt(2.0 / math.pi)
    return (0.5 * x) * (1.0 + jnp.tanh(x * (c + (c * 0.044715) * (x * x))))


def _dot(a, b):
    return jnp.dot(a, b, preferred_element_type=F32)


def _mxu_chunks(size, n):
    tiles = size // MXU_TILE
    assert tiles * MXU_TILE == size
    return [MXU_TILE * (tiles // n + (1 if i < tiles % n else 0)) for i in range(n)]


def _gmlp_in_kernel(h_ref, g_ref, w_ref, lng_ref, lnb_ref, u_ref, v_ref):
    e = u_ref.shape[-1]
    hn = _rms(h_ref[...], g_ref[...]).astype(BF16)
    v = _gelu(_dot(hn, w_ref[:, e:]))
    vc = v - jnp.mean(v, axis=-1, keepdims=True)
    var = jnp.mean(vc * vc, axis=-1, keepdims=True)
    v_ref[...] = (vc * lax.rsqrt(var + EPS) * lng_ref[...] + lnb_ref[...]).astype(BF16)
    u_ref[...] = _gelu(_dot(hn, w_ref[:, :e])).astype(BF16)


def _gmlp_out_kernel(u_ref, v_ref, ws_ref, bs_ref, wo_ref, h_ref, o_ref, y_scr):
    tc, e = u_ref.shape
    ng = ws_ref.shape[0]
    ge = e // ng
    for c in range(tc // CHUNK):
        r = slice(c * CHUNK, (c + 1) * CHUNK)
        for g in range(ng):
            cs = slice(g * ge, (g + 1) * ge)
            sv = _dot(ws_ref[g], v_ref[r, cs]) + bs_ref[:, g:g + 1]
            y_scr[r, cs] = (u_ref[r, cs].astype(F32) * sv).astype(BF16)
    o_ref[...] = h_ref[...] + _dot(y_scr[...], wo_ref[...])


def _mixer_gmlp(h, g, w_in, ln_g, ln_b, w_s, b_s, w_out, *, tm=512):
    n, d = h.shape
    e = w_out.shape[0]
    ng = w_s.shape[0]
    row = lambda i: (i, 0)
    u, v = pl.pallas_call(
        _gmlp_in_kernel,
        grid=(n // tm,),
        in_specs=[pl.BlockSpec((tm, d), row), _resident((1, d)), _resident((d, 2 * e)),
                  _resident((1, e)), _resident((1, e))],
        out_specs=[pl.BlockSpec((tm, e), row), pl.BlockSpec((tm, e), row)],
        out_shape=[jax.ShapeDtypeStruct((n, e), BF16)] * 2,
        compiler_params=_cparams("parallel"),
        name="gmlp_in",
    )(h, g.reshape(1, d), w_in.astype(BF16), ln_g.reshape(1, e), ln_b.reshape(1, e))
    mask = jnp.tril(jnp.ones((CHUNK, CHUNK), dtype=bool))
    ws = jnp.where(mask[None], w_s, 0.0).astype(BF16)
    return pl.pallas_call(
        _gmlp_out_kernel,
        grid=(n // tm,),
        in_specs=[pl.BlockSpec((tm, e), row), pl.BlockSpec((tm, e), row), _resident((ng, CHUNK, CHUNK)),
                  _resident((CHUNK, ng)), _resident((e, d)), pl.BlockSpec((tm, d), row)],
        out_specs=pl.BlockSpec((tm, d), row),
        out_shape=jax.ShapeDtypeStruct((n, d), F32),
        scratch_shapes=[pltpu.VMEM((tm, e), BF16)],
        compiler_params=_cparams("parallel"),
        name="gmlp_out",
    )(u, v, ws, b_s.T, w_out.astype(BF16), h)


def _ffn_kernel(h_ref, g_ref, wgu_ref, wd_ref, og_ref, o_ref, *, chunks, final):
    x = h_ref[...]
    hn = _rms(x, g_ref[...]).astype(BF16)
    f = wd_ref.shape[0]
    acc = x
    f0 = 0
    for fc in chunks:
        gg = _dot(hn, wgu_ref[:, f0:f0 + fc])
        uu = _dot(hn, wgu_ref[:, f + f0:f + f0 + fc])
        a = (gg * jax.nn.sigmoid(gg) * uu).astype(BF16)
        acc = acc + _dot(a, wd_ref[f0:f0 + fc, :])
        f0 += fc
    o_ref[...] = _rms(acc, og_ref[...]) if final else acc


def _ffn_dense(h, g, w_gu, w_down, out_g, final, *, tm=512):
    n, d = h.shape
    f = w_down.shape[0]
    row = lambda i: (i, 0)
    return pl.pallas_call(
        functools.partial(_ffn_kernel, chunks=_mxu_chunks(f, 2), final=final),
        grid=(n // tm,),
        in_specs=[pl.BlockSpec((tm, d), row), _resident((1, d)), _resident((d, 2 * f)), _resident((f, d)),
                  _resident((1, d))],
        out_specs=pl.BlockSpec((tm, d), row),
        out_shape=jax.ShapeDtypeStruct((n, d), F32),
        compiler_params=_cparams("parallel"),
        name="ffn_dense",
    )(h, g.reshape(1, d), w_gu.astype(BF16), w_down.astype(BF16), out_g.reshape(1, d))


def _route_kernel(h_ref, g_ref, wrt_ref, idx_ref, gate_ref):
    hn = _rms(h_ref[...], g_ref[...])
    logits = lax.dot_general(wrt_ref[...], hn, (((1,), (1,)), ((), ())), preferred_element_type=F32,
                             precision=lax.Precision.HIGHEST)
    ne = logits.shape[0]
    eid = lax.broadcasted_iota(jnp.int32, logits.shape, 0)
    neg = jnp.float32(-jnp.inf)
    m1 = jnp.max(logits, axis=0, keepdims=True)
    i1 = jnp.min(jnp.where(logits == m1, eid, ne), axis=0, keepdims=True)
    lg2 = jnp.where(eid == i1, neg, logits)
    m2 = jnp.max(lg2, axis=0, keepdims=True)
    i2 = jnp.min(jnp.where(lg2 == m2, eid, ne), axis=0, keepdims=True)
    ex = jnp.exp(m2 - m1)
    idx_ref[...] = jnp.concatenate([i1, i2], axis=0)
    gates = jnp.concatenate([1.0 / (1.0 + ex), ex / (1.0 + ex), jnp.zeros((LANES - TOP_K, ex.shape[1]), F32)], axis=0)
    gate_ref[...] = gates.T


def _dispatch_kernel(dst_ref, pad_ref, x_ref, o_hbm, zbuf, sem, zsem, *, tm):
    i = pl.program_id(0)
    rows = x_ref.shape[0]
    zrows = zbuf.shape[0]

    @pl.when(i == 0)
    def _():
        zbuf[...] = jnp.zeros_like(zbuf)
        for start_wait in (True, False):
            for e in range(pad_ref.shape[0]):
                first = pad_ref[e]

                @pl.when(first >= 0)
                def _():
                    base = pl.multiple_of(jnp.maximum(first, 0), zrows)
                    for q in range(tm // zrows):
                        cp = pltpu.make_async_copy(zbuf, o_hbm.at[pl.ds(base + q * zrows, zrows), :], zsem)
                        cp.start() if start_wait else cp.wait()

    def row_copy(r, slot_row):
        return pltpu.make_async_copy(x_ref.at[pl.ds(r, 1), :], o_hbm.at[pl.ds(slot_row, 1), :], sem)

    for r in range(rows):
        for k in range(TOP_K):
            row_copy(r, dst_ref[0, k, r]).start()
    for r in range(rows):
        for k in range(TOP_K):
            row_copy(r, 0).wait()


def _expert_kernel(te_ref, nu_ref, nv_ref, x_ref, g_ref, wg_ref, wu_ref, wd_ref, o_ref, xn_scr):
    i = pl.program_id(0)
    f = pl.program_id(1)
    used = i < nu_ref[0]
    half = o_ref.shape[0] // 2

    @pl.when(f == 0)
    def _():
        o_ref[...] = jnp.zeros_like(o_ref)

    @pl.when(jnp.logical_and(used, f == 0))
    def _():
        xn_scr[...] = _rms(x_ref[...], g_ref[...]).astype(BF16)

    def swiglu_rows(nrows):
        xn = xn_scr[:nrows, :]
        gg = _dot(xn, wg_ref[0, 0].astype(BF16))
        uu = _dot(xn, wu_ref[0, 0].astype(BF16))
        a = (gg * jax.nn.sigmoid(gg) * uu).astype(BF16)
        o_ref[:nrows, :] += _dot(a, wd_ref[0, 0].astype(BF16))

    @pl.when(jnp.logical_and(used, nv_ref[i] > half))
    def _():
        swiglu_rows(2 * half)

    @pl.when(jnp.logical_and(used, nv_ref[i] <= half))
    def _():
        swiglu_rows(half)


def _combine_kernel(cur_ref, nxt_ref, h_ref, gate_ref, y_hbm, og_ref, o_ref, buf, sems, *, final):
    i = pl.program_id(0)
    nstep = pl.num_programs(0)
    rows = h_ref.shape[0]

    def row_copy(src_row, par, k, r):
        return pltpu.make_async_copy(y_hbm.at[pl.ds(src_row, 1), :], buf.at[par, k, pl.ds(r, 1), :], sems.at[par])

    def issue(idx_ref, par):
        for r in range(rows):
            for k in range(TOP_K):
                row_copy(idx_ref[0, k, r], par, k, r).start()

    def drain(par):
        for r in range(rows):
            for k in range(TOP_K):
                row_copy(0, par, k, r).wait()

    @pl.when(i == 0)
    def _():
        issue(cur_ref, 0)

    for par in range(2):
        @pl.when(i % 2 == par)
        def _():
            @pl.when(i + 1 < nstep)
            def _():
                issue(nxt_ref, 1 - par)

            drain(par)
            gate = gate_ref[...]
            out = h_ref[...] + gate[:, 0:1] * buf[par, 0] + gate[:, 1:2] * buf[par, 1]
            o_ref[...] = _rms(out, og_ref[...]) if final else out


def _moe(h, g, layer, w_router, w_gu, w_down, out_g, final, *, tm=1024, tf=512, rows=256, drows=1024, zrows=256):
    n, d = h.shape
    ne = w_router.shape[1]
    f = w_down.shape[2]
    nf = f // tf
    row = lambda i: (i, 0)
    idx, gate = pl.pallas_call(
        _route_kernel,
        grid=(n // 512,),
        in_specs=[pl.BlockSpec((512, d), row), _resident((1, d)), _resident((ne, d))],
        out_specs=[pl.BlockSpec((TOP_K, 512), lambda i: (0, i)), pl.BlockSpec((512, LANES), row)],
        out_shape=[jax.ShapeDtypeStruct((TOP_K, n), jnp.int32), jax.ShapeDtypeStruct((n, LANES), F32)],
        compiler_params=_cparams("parallel"),
        name="moe_route",
    )(h, g.reshape(1, d), w_router.T)

    npair = n * TOP_K
    ntile = npair // tm + ne
    e_flat = idx.reshape(npair)
    onehot = (e_flat[:, None] == jnp.arange(ne, dtype=jnp.int32)[None, :]).astype(jnp.int32)
    csum = jnp.cumsum(onehot, axis=0)
    rank = jnp.sum((csum - onehot) * onehot, axis=1)
    counts = csum[-1]
    padded = ((counts + tm - 1) // tm) * tm
    ends = jnp.cumsum(padded)
    dest = (ends - padded)[e_flat] + rank
    tile_start = jnp.arange(ntile, dtype=jnp.int32) * tm
    spare = ends[-1] + tile_start[:ne]
    pad_tile = jnp.concatenate([jnp.where(padded > 0, ends - tm, -1),
                                jnp.where(spare < ntile * tm, spare, -1)]).astype(jnp.int32)
    tile_expert = jnp.minimum(jnp.sum((tile_start[:, None] >= ends[None, :]).astype(jnp.int32), axis=1), ne - 1)
    n_used = (ends[-1] // tm).astype(jnp.int32).reshape(1)
    in_group = tile_expert[:, None] == jnp.arange(ne, dtype=jnp.int32)[None, :]
    last_row = jnp.sum(jnp.where(in_group, (ends - padded + counts)[None, :], 0), axis=1)
    n_valid = jnp.clip(last_row - tile_start, 0, tm).astype(jnp.int32)

    nblk = n // rows
    pair_blk = lambda r: pl.BlockSpec((1, TOP_K, r), lambda i: (i, 0, 0), memory_space=pltpu.SMEM)
    by_block = lambda r: dest.reshape(TOP_K, n // r, r).transpose(1, 0, 2)
    xs = pl.pallas_call(
        functools.partial(_dispatch_kernel, tm=tm),
        grid=(n // drows,),
        in_specs=[pair_blk(drows), pl.BlockSpec(memory_space=pltpu.SMEM), pl.BlockSpec((drows, d), row)],
        out_specs=pl.BlockSpec(memory_space=pl.ANY),
        out_shape=jax.ShapeDtypeStruct((ntile * tm, d), F32),
        scratch_shapes=[pltpu.VMEM((zrows, d), F32), pltpu.SemaphoreType.DMA(()), pltpu.SemaphoreType.DMA(())],
        compiler_params=_cparams("arbitrary"),
        name="moe_dispatch",
    )(by_block(drows), pad_tile, h)
    dest = by_block(rows)

    def wsel(off):
        def index(i, j, te, nu, nv):
            return (layer, te[i], 0, off + jnp.where(i < nu[0], j, nf - 1))
        return index

    def wdsel(i, j, te, nu, nv):
        return (layer, te[i], jnp.where(i < nu[0], j, nf - 1), 0)

    ys = pl.pallas_call(
        _expert_kernel,
        grid_spec=pltpu.PrefetchScalarGridSpec(
            num_scalar_prefetch=3,
            grid=(ntile, nf),
            in_specs=[pl.BlockSpec((tm, d), lambda i, j, te, nu, nv: (jnp.where(i < nu[0], i, 0), 0)),
                      pl.BlockSpec((1, d), lambda i, j, te, nu, nv: (0, 0)),
                      pl.BlockSpec((1, 1, d, tf), wsel(0)),
                      pl.BlockSpec((1, 1, d, tf), wsel(nf)),
                      pl.BlockSpec((1, 1, tf, d), wdsel)],
            out_specs=pl.BlockSpec((tm, d), lambda i, j, te, nu, nv: (i, 0)),
            scratch_shapes=[pltpu.VMEM((tm, d), BF16)],
        ),
        out_shape=jax.ShapeDtypeStruct((ntile * tm, d), F32),
        compiler_params=_cparams("arbitrary", "arbitrary"),
        name="moe_expert",
    )(tile_expert, n_used, n_valid, xs, g.reshape(1, d), w_gu, w_gu, w_down)

    return pl.pallas_call(
        functools.partial(_combine_kernel, final=final),
        grid=(nblk,),
        in_specs=[pair_blk(rows),
                  pl.BlockSpec((1, TOP_K, rows), lambda i: (jnp.minimum(i + 1, nblk - 1), 0, 0),
                               memory_space=pltpu.SMEM),
                  pl.BlockSpec((rows, d), row), pl.BlockSpec((rows, LANES), row),
                  pl.BlockSpec(memory_space=pl.ANY), _resident((1, d))],
        out_specs=pl.BlockSpec((rows, d), row),
        out_shape=jax.ShapeDtypeStruct((n, d), F32),
        scratch_shapes=[pltpu.VMEM((2, TOP_K, rows, d), F32), pltpu.SemaphoreType.DMA((2,))],
        compiler_params=_cparams("arbitrary"),
        name="moe_combine",
    )(dest, dest, h, gate, ys, out_g.reshape(1, d))


def _to_time_major(x_ref, slab):
    nb, tt, d = x_ref.shape
    for b in range(nb):
        for j in range(d // LANES):
            slab[j, pl.ds(b, tt, stride=nb), :] = x_ref[b, :, j * LANES:(j + 1) * LANES]
    return jnp.concatenate([slab[j] for j in range(d // LANES)], axis=1)


def _residual_from_time_major(y, x_ref, o_ref, slab):
    nb, tt, d = x_ref.shape
    for j in range(d // LANES):
        slab[j] = y[:, j * LANES:(j + 1) * LANES]
    for b in range(nb):
        for j in range(d // LANES):
            cs = slice(j * LANES, (j + 1) * LANES)
            o_ref[b, :, cs] = x_ref[b, :, cs] + slab[j, pl.ds(b, tt, stride=nb), :]


def _s5_prep_kernel(ar_ref, ai_ref, ld_ref, brt_ref, bit_ref, bbr_ref, bbi_ref, lr_ref, li_ref):
    ar = ar_ref[...]
    ai = ai_ref[...]
    dt = jnp.exp(ld_ref[...])
    mag = jnp.exp(ar * dt)
    lr = mag * jnp.cos(ai * dt)
    li = mag * jnp.sin(ai * dt)
    lr_ref[...] = lr
    li_ref[...] = li
    den = ar * ar + ai * ai
    nr = lr - 1.0
    kr = ((nr * ar + li * ai) / den)[:, None, :]
    ki = ((li * ar - nr * ai) / den)[:, None, :]
    brt = brt_ref[...]
    bit = bit_ref[...]
    bbr_ref[...] = kr * brt - ki * bit
    bbi_ref[...] = kr * bit + ki * brt


def _s5_kernel(h_ref, g_ref, bre_ref, bim_ref, cre_ref, cim_ref, lr_ref, li_ref, d_ref, w_ref, o_ref,
               slab_in, slab_out, s_re, s_im, c_re, c_im):
    nb, tt, d = h_ref.shape
    nslab, _, sw = bre_ref.shape

    @pl.when(pl.program_id(0) == 0)
    def _():
        c_re[...] = jnp.zeros_like(c_re)
        c_im[...] = jnp.zeros_like(c_im)

    hn = _rms(_to_time_major(h_ref, slab_in), g_ref[...])
    hb = hn.astype(BF16)
    for j in range(nslab):
        xj = hb[:, j * LANES:(j + 1) * LANES]
        s_re[j] = _dot(xj, bre_ref[j])
        s_im[j] = _dot(xj, bim_ref[j])

    ys = []
    for j in range(nslab):
        lr = jnp.broadcast_to(lr_ref[j:j + 1, :], (nb, sw))
        li = jnp.broadcast_to(li_ref[j:j + 1, :], (nb, sw))

        def step(t, carry, j=j, lr=lr, li=li):
            cr, ci = carry
            rows = pl.ds(pl.multiple_of(t * nb, nb), nb)
            nr = lr * cr - li * ci + s_re[j, rows, :]
            ni = lr * ci + li * cr + s_im[j, rows, :]
            s_re[j, rows, :] = nr
            s_im[j, rows, :] = ni
            return nr, ni

        cr, ci = lax.fori_loop(0, tt, step, (c_re[j], c_im[j]), unroll=True)
        c_re[j] = cr
        c_im[j] = ci
        ys.append(_dot(s_re[j].astype(BF16), cre_ref[j]) - _dot(s_im[j].astype(BF16), cim_ref[j]))

    y = jnp.concatenate(ys, axis=1) + d_ref[...] * hn
    ab = _dot(_gelu(y).astype(BF16), w_ref[...])
    _residual_from_time_major(ab[:, :d] * jax.nn.sigmoid(ab[:, d:]), h_ref, o_ref, slab_out)


def _block_diag(blocks, per):
    n, a, b = blocks.shape
    eye = jnp.eye(per, dtype=blocks.dtype)
    out = blocks.reshape(n // per, per, a, 1, b) * eye[None, :, None, :, None]
    return out.reshape(n // per, per * a, per * b)


def _mixer_s5(h, g, bsz, a_re, a_im, log_dt, b_re, b_im, c_re, c_im, d_skip, w_glu, *, tt=64):
    n, d = h.shape
    seq = n // bsz
    ng, p = a_re.shape
    per = LANES // S5_GROUP
    nslab = ng // per
    sw = per * p
    full = lambda *shape: pl.BlockSpec(shape, lambda i: (0,) * len(shape))
    bbr, bbi, lr, li = pl.pallas_call(
        _s5_prep_kernel,
        grid=(1,),
        in_specs=[full(ng, p)] * 3 + [full(ng, S5_GROUP, p)] * 2,
        out_specs=[full(ng, S5_GROUP, p)] * 2 + [full(ng, p)] * 2,
        out_shape=[jax.ShapeDtypeStruct((ng, S5_GROUP, p), F32)] * 2 + [jax.ShapeDtypeStruct((ng, p), F32)] * 2,
        name="s5_prep",
    )(a_re, a_im, jnp.broadcast_to(log_dt[:, None], (ng, p)), jnp.swapaxes(b_re, 1, 2), jnp.swapaxes(b_im, 1, 2))
    bre = _block_diag(bbr, per).astype(BF16)
    bim = _block_diag(bbi, per).astype(BF16)
    cre = _block_diag(jnp.swapaxes(c_re, 1, 2), per).astype(BF16)
    cim = _block_diag(jnp.swapaxes(c_im, 1, 2), per).astype(BF16)
    blk = pl.BlockSpec((bsz, tt, d), lambda i: (0, i, 0))
    out = pl.pallas_call(
        _s5_kernel,
        grid=(seq // tt,),
        in_specs=[blk, _resident((1, d)), _resident((nslab, LANES, sw)), _resident((nslab, LANES, sw)),
                  _resident((nslab, sw, LANES)), _resident((nslab, sw, LANES)), _resident((nslab, sw)),
                  _resident((nslab, sw)), _resident((1, d)), _resident((d, 2 * d))],
        out_specs=blk,
        out_shape=jax.ShapeDtypeStruct((bsz, seq, d), F32),
        scratch_shapes=[pltpu.VMEM((d // LANES, tt * bsz, LANES), F32)] * 2
        + [pltpu.VMEM((nslab, tt * bsz, sw), F32)] * 2 + [pltpu.VMEM((nslab, bsz, sw), F32)] * 2,
        compiler_params=_cparams("arbitrary"),
        name="s5_mixer",
    )(h.reshape(bsz, seq, d), g.reshape(1, d), bre, bim, cre, cim, lr.reshape(nslab, sw), li.reshape(nslab, sw),
      d_skip.reshape(1, d), w_glu.astype(BF16))
    return out.reshape(n, d)


def _rglru_kernel(h_ref, g_ref, win_ref, cw_ref, cb_ref, wr_ref, br_ref, wi_ref, bi_ref, lam_ref, wout_ref, o_ref,
                  slab_in, slab_out, hist, state, a_scr, b_scr):
    nb, tt, d = h_ref.shape
    r = wout_ref.shape[0]
    kw = cw_ref.shape[0]
    nh = wr_ref.shape[0]
    hd = r // nh
    rows = tt * nb

    @pl.when(pl.program_id(0) == 0)
    def _():
        hist[...] = jnp.zeros_like(hist)
        state[...] = jnp.zeros_like(state)

    hn = _rms(_to_time_major(h_ref, slab_in), g_ref[...]).astype(BF16)
    xb = _dot(hn, win_ref[:, :r])
    gate = _gelu(_dot(hn, win_ref[:, r:]))
    xpad = jnp.concatenate([hist[...], xb], axis=0)
    hist[...] = xb[rows - (kw - 1) * nb:]
    xc = cb_ref[...]
    for k in range(kw):
        xc = xc + cw_ref[k:k + 1, :] * xpad[k * nb:k * nb + rows]
    lam = -lam_ref[...]
    sp = jnp.maximum(lam, 0.0) + jnp.log1p(jnp.exp(-jnp.abs(lam)))
    for hh in range(nh):
        cs = slice(hh * hd, (hh + 1) * hd)
        xh = xc[:, cs]
        xhb = xh.astype(BF16)
        rg = jax.nn.sigmoid(_dot(xhb, wr_ref[hh]) + br_ref[:, cs])
        ig = jax.nn.sigmoid(_dot(xhb, wi_ref[hh]) + bi_ref[:, cs])
        a = jnp.exp((-LRU_C) * rg * sp[:, cs])
        a_scr[:, cs] = a
        b_scr[:, cs] = jnp.sqrt(1.0 - a * a) * (ig * xh)

    def step(t, hcur):
        sl = pl.ds(pl.multiple_of(t * nb, nb), nb)
        hnew = a_scr[sl, :] * hcur + b_scr[sl, :]
        b_scr[sl, :] = hnew
        return hnew

    state[...] = lax.fori_loop(0, tt, step, state[...], unroll=True)
    y = (b_scr[...] * gate).astype(BF16)
    _residual_from_time_major(_dot(y, wout_ref[...]), h_ref, o_ref, slab_out)


def _mixer_rglru(h, g, bsz, w_in, conv_w, conv_b, w_r, b_r, w_i, b_i, lam, w_out, *, tt=64):
    n, d = h.shape
    seq = n // bsz
    r = w_out.shape[0]
    nh, hd, _ = w_r.shape
    kw = conv_w.shape[0]
    blk = pl.BlockSpec((bsz, tt, d), lambda i: (0, i, 0))
    out = pl.pallas_call(
        _rglru_kernel,
        grid=(seq // tt,),
        in_specs=[blk, _resident((1, d)), _resident((d, 2 * r)), _resident((kw, r)), _resident((1, r)),
                  _resident((nh, hd, hd)), _resident((1, r)), _resident((nh, hd, hd)), _resident((1, r)),
                  _resident((1, r)), _resident((r, d))],
        out_specs=blk,
        out_shape=jax.ShapeDtypeStruct((bsz, seq, d), F32),
        scratch_shapes=[pltpu.VMEM((d // LANES, tt * bsz, LANES), F32)] * 2
        + [pltpu.VMEM(((kw - 1) * bsz, r), F32), pltpu.VMEM((bsz, r), F32),
           pltpu.VMEM((tt * bsz, r), F32), pltpu.VMEM((tt * bsz, r), F32)],
        compiler_params=_cparams("arbitrary"),
        name="rglru_mixer",
    )(h.reshape(bsz, seq, d), g.reshape(1, d), w_in.astype(BF16), conv_w, conv_b.reshape(1, r), w_r.astype(BF16),
      b_r.reshape(1, r), w_i.astype(BF16), b_i.reshape(1, r), lam.reshape(1, r), w_out.astype(BF16))
    return out.reshape(n, d)


def kernel(x, norm_mix_g, norm_ffn_g, norm_final_g, a_w_in, a_ln_g, a_ln_b, a_w_s, a_b_s, a_w_out, b_a_re, b_a_im, b_log_dt, b_b_re, b_b_im, b_c_re, b_c_im, b_d, b_w_glu, c_w_in, c_conv_w, c_conv_b, c_w_r, c_b_r, c_w_i, c_b_i, c_lam, c_w_out, f_w_gu, f_w_down, m_w_router, m_w_gu, m_w_down):
    bsz, seq, d = x.shape
    depth = norm_mix_g.shape[0]
    h = x.reshape(bsz * seq, d)
    for i in range(depth):
        kind = i % N_MIXERS
        j = i // N_MIXERS
        if kind == 0:
            h = _mixer_gmlp(h, norm_mix_g[i], a_w_in[j], a_ln_g[j], a_ln_b[j], a_w_s[j], a_b_s[j], a_w_out[j])
        elif kind == 1:
            h = _mixer_s5(h, norm_mix_g[i], bsz, b_a_re[j], b_a_im[j], b_log_dt[j], b_b_re[j], b_b_im[j],
                          b_c_re[j], b_c_im[j], b_d[j], b_w_glu[j])
        else:
            h = _mixer_rglru(h, norm_mix_g[i], bsz, c_w_in[j], c_conv_w[j], c_conv_b[j], c_w_r[j], c_b_r[j],
                             c_w_i[j], c_b_i[j], c_lam[j], c_w_out[j])
        k = i // 2
        final = i == depth - 1
        if i % 2 == 0:
            h = _ffn_dense(h, norm_ffn_g[i], f_w_gu[k], f_w_down[k], norm_final_g, final)
        else:
            h = _moe(h, norm_ffn_g[i], k, m_w_router[k], m_w_gu, m_w_down, norm_final_g, final)
    return h.reshape(bsz, seq, d)
```
